```python
import math
import jax, jax.numpy as jnp
from jax import lax
import numpy as np

D_MODEL = 1024
BATCH = 4
SEQ = 8192
DEPTH = 4

N_MEM = 256
POOL_WIDTH = 256
POOL_GROUPS = 4
POOL_GROUP_DIM = POOL_WIDTH // POOL_GROUPS
POOL_WINDOWS = (2, 4, 8, 16)
DA_HEADS = 4
DA_HEAD_DIM = 64
DA_V_DIM = 2 * DA_HEAD_DIM
DA_WIDTH = DA_HEADS * DA_V_DIM
CA_HEADS = 4
CA_HEAD_DIM = 64
CA_WIDTH = CA_HEADS * CA_HEAD_DIM
IN_SPLITS = (POOL_WIDTH, POOL_WIDTH + DA_WIDTH, POOL_WIDTH + 2 * DA_WIDTH, POOL_WIDTH + 3 * DA_WIDTH)
IN_WIDTH = POOL_WIDTH + 3 * DA_WIDTH + CA_WIDTH
N_BRANCH = 3
D_FF = 2816
ROPE_THETA = 500000.0
ROPE_DIM = DA_HEAD_DIM // 4
Q_BLOCK = 128
NORM_EPS = 1e-6
LAMBDA_STD = 0.1

kernel_name = "hybrid_pool_diffattn_memxattn_macaron_encoder"


def rms_norm(x, g):
    xf = x.astype(jnp.float32)
    xf = xf * lax.rsqrt(jnp.mean(xf * xf, axis=-1, keepdims=True) + NORM_EPS)
    return (xf * g.astype(jnp.float32)).astype(x.dtype)


def swiglu(h, w_up, w_down):
    a, b = jnp.split(h @ w_up, 2, axis=-1)
    return (jax.nn.silu(a) * b) @ w_down


def rope_tables(positions):
    inv = ROPE_THETA ** (-jnp.arange(0, ROPE_DIM, 2, dtype=jnp.float32) / ROPE_DIM)
    ang = positions.astype(jnp.float32)[..., None] * inv
    return jnp.cos(ang), jnp.sin(ang)


def apply_partial_rope(x, cos, sin):
    half = ROPE_DIM // 2
    x1, x2, xp = x[..., :half], x[..., half:ROPE_DIM], x[..., ROPE_DIM:]
    c = cos[:, :, None, None, :].astype(x.dtype)
    s = sin[:, :, None, None, :].astype(x.dtype)
    return jnp.concatenate([x1 * c - x2 * s, x2 * c + x1 * s, xp], axis=-1)


def multiscale_pool(u, w_group, scale):
    B, S, _ = u.shape
    uf = u.astype(jnp.float32)
    csum = jnp.concatenate([jnp.zeros_like(uf[:, :1]), jnp.cumsum(uf, axis=1)], axis=1)
    pos = jnp.arange(S)
    outs = []
    for gi, w in enumerate(POOL_WINDOWS):
        lo = jnp.clip(pos - w // 2, 0, S)
        hi = jnp.clip(pos + w // 2, 0, S)
        sl = slice(gi * POOL_GROUP_DIM, (gi + 1) * POOL_GROUP_DIM)
        cg = csum[:, :, sl]
        cnt = (hi - lo).astype(jnp.float32)[None, :, None]
        outs.append((cg[:, hi] - cg[:, lo]) / cnt - uf[:, :, sl])
    pooled = jnp.stack(outs, axis=2).astype(u.dtype)
    mixed = jnp.einsum('bsgc,gcd->bsgd', pooled, w_group)
    return mixed.reshape(B, S, POOL_WIDTH) * scale


def diff_attention(q, k, v, lam):
    B, S = q.shape[:2]
    nblk = S // Q_BLOCK
    qb = q.reshape(B, nblk, Q_BLOCK, DA_HEADS, 2, DA_HEAD_DIM).transpose(1, 0, 2, 3, 4, 5)
    scale = DA_HEAD_DIM ** -0.5

    def block(qi):
        s = jnp.einsum('bqhcd,bkhcd->bhcqk', qi, k, preferred_element_type=jnp.float32) * scale
        p = jax.nn.softmax(s, axis=-1)
        a = p[:, :, 0] - lam * p[:, :, 1]
        return jnp.einsum('bhqk,bkhe->bqhe', a.astype(v.dtype), v)

    o = lax.map(block, qb)
    return o.transpose(1, 0, 2, 3, 4).reshape(B, S, DA_HEADS, DA_V_DIM)


def memory_cross_attention(q, kv):
    B, M, _ = kv.shape
    k, v = jnp.split(kv, 2, axis=-1)
    k = k.reshape(B, M, CA_HEADS, CA_HEAD_DIM)
    v = v.reshape(B, M, CA_HEADS, CA_HEAD_DIM)
    s = jnp.einsum('bshd,bmhd->bhsm', q, k, preferred_element_type=jnp.float32) * (CA_HEAD_DIM ** -0.5)
    p = jax.nn.softmax(s, axis=-1).astype(v.dtype)
    return jnp.einsum('bhsm,bmhd->bshd', p, v)


def setup_inputs(seed: int = 0) -> dict:
    key = jax.random.key(seed)
    ks = jax.random.split(key, 32)
    f32 = jnp.float32

    def nrm(k, shape, fan_in):
        return jax.random.normal(k, shape, f32) * (fan_in ** -0.5)

    def gain(k, shape):
        return 1.0 + 0.05 * jax.random.normal(k, shape, f32)

    L, D = DEPTH, D_MODEL
    return {
        "x": jax.random.normal(ks[0], (BATCH, SEQ, D), f32),
        "mem": jax.random.normal(ks[1], (BATCH, N_MEM, D), f32),
        "positions": jnp.broadcast_to(jnp.arange(SEQ, dtype=jnp.int32), (BATCH, SEQ)),
        "ffn1_pre_g": gain(ks[2], (L, D)),
        "ffn1_w_up": nrm(ks[3], (L, D, 2 * D_FF), D),
        "ffn1_w_down": nrm(ks[4], (L, D_FF, D), D_FF),
        "ffn1_post_g": gain(ks[5], (L, D)),
        "mix_pre_g": gain(ks[6], (L, D)),
        "w_in": nrm(ks[7], (L, D, IN_WIDTH), D),
        "pool_w": nrm(ks[8], (L, POOL_GROUPS, POOL_GROUP_DIM, POOL_GROUP_DIM), POOL_GROUP_DIM),
        "pool_scale": gain(ks[9], (L, POOL_WIDTH)),
        "da_lambda_q1": LAMBDA_STD * jax.random.normal(ks[10], (L, DA_HEAD_DIM), f32),
        "da_lambda_k1": LAMBDA_STD * jax.random.normal(ks[11], (L, DA_HEAD_DIM), f32),
        "da_lambda_q2": LAMBDA_STD * jax.random.normal(ks[12], (L, DA_HEAD_DIM), f32),
        "da_lambda_k2": LAMBDA_STD * jax.random.normal(ks[13], (L, DA_HEAD_DIM), f32),
        "da_subln_g": gain(ks[14], (L, DA_V_DIM)),
        "mem_norm_g": gain(ks[15], (L, D)),
        "w_mem_kv": nrm(ks[16], (L, D, 2 * CA_WIDTH), D),
        "w_gate": nrm(ks[17], (L, D, N_BRANCH * D), D),
        "b_gate": 0.01 * jax.random.normal(ks[18], (L, N_BRANCH * D), f32),
        "w_br_pool": nrm(ks[19], (L, POOL_WIDTH, D), POOL_WIDTH),
        "w_br_da": nrm(ks[20], (L, DA_WIDTH, D), DA_WIDTH),
        "w_br_ca": nrm(ks[21], (L, CA_WIDTH, D), CA_WIDTH),
        "w_out": nrm(ks[22], (L, D, D), D),
        "mix_post_g": gain(ks[23], (L, D)),
        "ffn2_pre_g": gain(ks[24], (L, D)),
        "ffn2_w_up": nrm(ks[25], (L, D, 2 * D_FF), D),
        "ffn2_w_down": nrm(ks[26], (L, D_FF, D), D_FF),
        "ffn2_post_g": gain(ks[27], (L, D)),
    }


def reference(x, mem, positions, ffn1_pre_g, ffn1_w_up, ffn1_w_down, ffn1_post_g,
              mix_pre_g, w_in, pool_w, pool_scale, da_lambda_q1, da_lambda_k1,
              da_lambda_q2, da_lambda_k2, da_subln_g, mem_norm_g, w_mem_kv,
              w_gate, b_gate, w_br_pool, w_br_da, w_br_ca, w_out, mix_post_g,
              ffn2_pre_g, ffn2_w_up, ffn2_w_down, ffn2_post_g):
    B, S, _ = x.shape
    cos, sin = rope_tables(positions)
    f32 = jnp.float32
    for l in range(DEPTH):
        lam_init = 0.8 - 0.6 * math.exp(-0.3 * l)

        h = rms_norm(x, ffn1_pre_g[l])
        x = x + 0.5 * rms_norm(swiglu(h, ffn1_w_up[l], ffn1_w_down[l]), ffn1_post_g[l])

        h = rms_norm(x, mix_pre_g[l])
        u_pool, q_da, k_da, v_da, q_ca = jnp.split(h @ w_in[l], IN_SPLITS, axis=-1)

        y_pool = multiscale_pool(u_pool, pool_w[l], pool_scale[l])

        q = apply_partial_rope(q_da.reshape(B, S, DA_HEADS, 2, DA_HEAD_DIM), cos, sin)
        k = apply_partial_rope(k_da.reshape(B, S, DA_HEADS, 2, DA_HEAD_DIM), cos, sin)
        v = v_da.reshape(B, S, DA_HEADS, DA_V_DIM)
        lam = (jnp.exp(jnp.sum(da_lambda_q1[l].astype(f32) * da_lambda_k1[l].astype(f32)))
               - jnp.exp(jnp.sum(da_lambda_q2[l].astype(f32) * da_lambda_k2[l].astype(f32)))
               + lam_init)
        o = diff_attention(q, k, v, lam)
        y_da = (rms_norm(o, da_subln_g[l]) * (1.0 - lam_init)).reshape(B, S, DA_WIDTH)

        kv = rms_norm(mem, mem_norm_g[l]) @ w_mem_kv[l]
        y_ca = memory_cross_attention(q_ca.reshape(B, S, CA_HEADS, CA_HEAD_DIM), kv).reshape(B, S, CA_WIDTH)

        g_pool, g_da, g_ca = jnp.split(jax.nn.sigmoid(h @ w_gate[l] + b_gate[l]), N_BRANCH, axis=-1)
        merged = (g_pool * (y_pool @ w_br_pool[l])
                  + g_da * (y_da @ w_br_da[l])
                  + g_ca * (y_ca @ w_br_ca[l]))
        x = x + rms_norm(merged @ w_out[l], mix_post_g[l])

        h = rms_norm(x, ffn2_pre_g[l])
        x = x + 0.5 * rms_norm(swiglu(h, ffn2_w_up[l], ffn2_w_down[l]), ffn2_post_g[l])
    return x
```

```python
import functools
import math

import jax
import jax.numpy as jnp
import numpy as np
from jax import lax
from jax.experimental import pallas as pl
from jax.experimental.pallas import tpu as pltpu

F32 = jnp.float32
BF16 = jnp.bfloat16

POOL_WINDOWS = (2, 4, 8, 16)
POOL_WIDTH = 256
POOL_GROUP_DIM = POOL_WIDTH // len(POOL_WINDOWS)
POOL_HALO = max(POOL_WINDOWS) // 2
DA_HEADS = 4
DA_HEAD_DIM = 64
DA_V_DIM = 2 * DA_HEAD_DIM
DA_WIDTH = DA_HEADS * DA_V_DIM
CA_HEADS = 4
CA_HEAD_DIM = 64
CA_WIDTH = CA_HEADS * CA_HEAD_DIM
ROPE_THETA = 500000.0
ROPE_DIM = DA_HEAD_DIM // 4
ROPE_HALF = ROPE_DIM // 2
NORM_EPS = 1e-6
LOG2E = math.log2(math.e)

V7X_LANES = 128
V7X_SUBLANES = 8
V7X_VMEM_LIMIT_BYTES = 60000 * 1024

TOKEN_TILE = 512
ATTN_Q_TILE = 512
ATTN_K_TILE = 512


def _compiler_params(n_grid_axes):
    return pltpu.CompilerParams(
        dimension_semantics=("arbitrary",) * n_grid_axes,
        vmem_limit_bytes=V7X_VMEM_LIMIT_BYTES,
    )


def _resident(block_shape, index_map):
    return pl.BlockSpec(block_shape, index_map, pipeline_mode=pl.Buffered(1))


def _rms_norm(x, g):
    ms = jnp.mean(x * x, axis=-1, keepdims=True)
    return x * lax.rsqrt(ms + NORM_EPS) * g


def _dot(a, b):
    return jnp.dot(a, b, preferred_element_type=F32)


def _rope_table_kernel(pos_ref, cos_ref, sin_ref):
    pos = pos_ref[...].astype(F32)
    lane = lax.broadcasted_iota(jnp.int32, pos.shape, 1)
    in_comp = lane % DA_HEAD_DIM
    freq_idx = lane % ROPE_HALF
    inv_freq = (np.float32(ROPE_THETA) ** (-np.arange(0, ROPE_DIM, 2, dtype=np.float32) / np.float32(ROPE_DIM)))
    inv = jnp.full(pos.shape, float(inv_freq[0]), F32)
    for j in range(1, ROPE_HALF):
        inv = jnp.where(freq_idx == j, float(inv_freq[j]), inv)
    ang = pos * inv
    rotary = in_comp < ROPE_DIM
    cos_ref[...] = jnp.where(rotary, jnp.cos(ang), 1.0)
    sin_v = jnp.sin(ang)
    sin_ref[...] = jnp.where(rotary, jnp.where(in_comp < ROPE_HALF, -sin_v, sin_v), 0.0)


def _rope_tables(positions):
    n = positions.size
    pos = jnp.broadcast_to(positions.reshape(n, 1), (n, V7X_LANES))
    spec = pl.BlockSpec((TOKEN_TILE, V7X_LANES), lambda i: (i, 0))
    return pl.pallas_call(
        _rope_table_kernel,
        grid=(n // TOKEN_TILE,),
        in_specs=[spec],
        out_specs=[spec, spec],
        out_shape=[jax.ShapeDtypeStruct((n, V7X_LANES), F32)] * 2,
        compiler_params=_compiler_params(1),
        name="rope_tables",
    )(pos)


def _ffn_kernel(x_ref, g_pre_ref, w_up_ref, w_down_ref, g_post_ref, o_ref, *, d_ff):
    x = x_ref[...]
    h = _rms_norm(x, g_pre_ref[...]).astype(BF16)
    ab = _dot(h, w_up_ref[...])
    a = ab[:, :d_ff]
    b = ab[:, d_ff:]
    act = (a * jax.nn.sigmoid(a) * b).astype(BF16)
    y = _dot(act, w_down_ref[...])
    o_ref[...] = x + 0.5 * _rms_norm(y, g_post_ref[...])


def _ffn(x, g_pre, w_up, w_down, g_post, layer):
    n, d = x.shape
    d_ff = w_down.shape[1]
    row = pl.BlockSpec((TOKEN_TILE, d), lambda i: (i, 0))
    gain = _resident((None, 1, d), lambda i: (layer, 0, 0))
    return pl.pallas_call(
        functools.partial(_ffn_kernel, d_ff=d_ff),
        grid=(n // TOKEN_TILE,),
        in_specs=[
            row,
            gain,
            _resident((None, d, 2 * d_ff), lambda i: (layer, 0, 0)),
            _resident((None, d_ff, d), lambda i: (layer, 0, 0)),
            gain,
        ],
        out_specs=row,
        out_shape=jax.ShapeDtypeStruct((n, d), F32),
        compiler_params=_compiler_params(1),
        name="ffn",
    )(x, g_pre, w_up, w_down, g_post)


def _apply_rope(t, cos_t, sin_t):
    width = t.shape[-1]
    lane = lax.broadcasted_iota(jnp.int32, t.shape, 1) % DA_HEAD_DIM
    partner = jnp.where(lane < ROPE_HALF,
                        pltpu.roll(t, width - ROPE_HALF, 1),
                        pltpu.roll(t, ROPE_HALF, 1))
    reps = width // cos_t.shape[-1]
    cos_w = jnp.concatenate([cos_t] * reps, axis=-1)
    sin_w = jnp.concatenate([sin_t] * reps, axis=-1)
    return t * cos_w + partner * sin_w


def _inproj_kernel(x_ref, g_ref, w_ref, cos_ref, sin_ref,
                   u_ref, qt_ref, k_ref, vt_ref, qca_ref):
    h = _rms_norm(x_ref[...], g_ref[...]).astype(BF16)
    proj = _dot(h, w_ref[...])
    o_q = POOL_WIDTH
    o_k = o_q + DA_WIDTH
    o_v = o_k + DA_WIDTH
    o_ca = o_v + DA_WIDTH
    cos_t = cos_ref[...]
    sin_t = sin_ref[...]
    u_ref[...] = proj[:, :o_q]
    q = _apply_rope(proj[:, o_q:o_k], cos_t, sin_t) * (DA_HEAD_DIM ** -0.5 * LOG2E)
    qt_ref[...] = q.T.astype(BF16)
    k_ref[...] = _apply_rope(proj[:, o_k:o_v], cos_t, sin_t).astype(BF16)
    vt_ref[...] = proj[:, o_v:o_ca].T.astype(BF16)
    qca_ref[...] = (proj[:, o_ca:] * (CA_HEAD_DIM ** -0.5 * LOG2E)).astype(BF16)


def _inproj(x, g, w_in, cos_t, sin_t, layer, batch, seq):
    n, d = x.shape
    tiles_per_seq = seq // TOKEN_TILE
    row = lambda width: pl.BlockSpec((TOKEN_TILE, width), lambda i: (i, 0))
    col = pl.BlockSpec((None, DA_WIDTH, TOKEN_TILE), lambda i: (i // tiles_per_seq, 0, i % tiles_per_seq))
    return pl.pallas_call(
        _inproj_kernel,
        grid=(n // TOKEN_TILE,),
        in_specs=[
            row(d),
            _resident((None, 1, d), lambda i: (layer, 0, 0)),
            _resident((None, d, w_in.shape[-1]), lambda i: (layer, 0, 0)),
            row(V7X_LANES),
            row(V7X_LANES),
        ],
        out_specs=[row(POOL_WIDTH), col, row(DA_WIDTH), col, row(CA_WIDTH)],
        out_shape=[
            jax.ShapeDtypeStruct((n, POOL_WIDTH), F32),
            jax.ShapeDtypeStruct((batch, DA_WIDTH, seq), BF16),
            jax.ShapeDtypeStruct((n, DA_WIDTH), BF16),
            jax.ShapeDtypeStruct((batch, DA_WIDTH, seq), BF16),
            jax.ShapeDtypeStruct((n, CA_WIDTH), BF16),
        ],
        compiler_params=_compiler_params(1),
        name="inproj",
    )(x, g, w_in, cos_t, sin_t)


def _attn_kernel(qt_ref, k_ref, vt_ref, lq1_ref, lk1_ref, lq2_ref, lk2_ref, g_ref, o_ref,
                 *, lam_init, seq):
    qt = qt_ref[...]
    tq = qt.shape[1]
    comp_row = lax.broadcasted_iota(jnp.int32, qt.shape, 0) < DA_HEAD_DIM
    zero = jnp.zeros_like(qt)
    q_comp = (jnp.where(comp_row, qt, zero), jnp.where(comp_row, zero, qt))

    def step(j, carry):
        start = pl.multiple_of(j * ATTN_K_TILE, ATTN_K_TILE)
        k_blk = k_ref[pl.ds(start, ATTN_K_TILE), :]
        vt_blk = vt_ref[:, pl.ds(start, ATTN_K_TILE)]
        new = []
        for c in range(2):
            m, l, acc = carry[c]
            s = _dot(k_blk, q_comp[c])
            m_new = jnp.maximum(m, jnp.max(s, axis=0, keepdims=True))
            p = jnp.exp2(s - m_new)
            alpha = jnp.exp2(m - m_new)
            l = alpha * l + jnp.sum(p, axis=0, keepdims=True)
            acc = alpha * acc + _dot(vt_blk, p.astype(BF16))
            new.append((m_new, l, acc))
        return tuple(new)

    init = tuple((jnp.full((1, tq), -jnp.inf, F32), jnp.zeros((1, tq), F32),
                  jnp.zeros((DA_V_DIM, tq), F32)) for _ in range(2))
    (_, l0, acc0), (_, l1, acc1) = lax.fori_loop(0, seq // ATTN_K_TILE, step, init)

    lam = (jnp.exp(jnp.sum(lq1_ref[...] * lk1_ref[...], axis=-1, keepdims=True))
           - jnp.exp(jnp.sum(lq2_ref[...] * lk2_ref[...], axis=-1, keepdims=True))
           + lam_init)
    o_t = acc0 * (1.0 / l0) - lam * (acc1 * (1.0 / l1))
    o = _rms_norm(o_t.T, g_ref[...]) * (1.0 - lam_init)
    o_ref[...] = o.astype(BF16)


def _attention(qt, k, vt, lq1, lk1, lq2, lk2, g, layer, lam_init):
    batch, _, seq = qt.shape
    k = k.reshape(batch, seq, DA_WIDTH)
    small = lambda width: _resident((None, 1, width), lambda b, h, i: (layer, 0, 0))
    out = pl.pallas_call(
        functools.partial(_attn_kernel, lam_init=lam_init, seq=seq),
        grid=(batch, DA_HEADS, seq // ATTN_Q_TILE),
        in_specs=[
            pl.BlockSpec((None, DA_V_DIM, ATTN_Q_TILE), lambda b, h, i: (b, h, i)),
            pl.BlockSpec((None, seq, DA_V_DIM), lambda b, h, i: (b, 0, h)),
            pl.BlockSpec((None, DA_V_DIM, seq), lambda b, h, i: (b, h, 0)),
            small(DA_HEAD_DIM), small(DA_HEAD_DIM), small(DA_HEAD_DIM), small(DA_HEAD_DIM),
            small(DA_V_DIM),
        ],
        out_specs=pl.BlockSpec((None, ATTN_Q_TILE, DA_V_DIM), lambda b, h, i: (b, i, h)),
        out_shape=jax.ShapeDtypeStruct((batch, seq, DA_WIDTH), BF16),
        compiler_params=_compiler_params(3),
        name="diff_attention",
    )(qt, k, vt, lq1, lk1, lq2, lk2, g)
    return out.reshape(batch * seq, DA_WIDTH)


def _mem_kv_kernel(mem_ref, g_ref, w_ref, kt_ref, v_ref):
    hm = _rms_norm(mem_ref[...], g_ref[...]).astype(BF16)
    kv = _dot(hm, w_ref[...])
    k_t = kv[:, :CA_WIDTH].T
    v = kv[:, CA_WIDTH:]
    k_head = lax.broadcasted_iota(jnp.int32, k_t.shape, 0) // CA_HEAD_DIM
    v_head = lax.broadcasted_iota(jnp.int32, v.shape, 1) // CA_HEAD_DIM
    for h in range(CA_HEADS):
        kt_ref[h] = jnp.where(k_head == h, k_t, 0.0).astype(BF16)
        v_ref[h] = jnp.where(v_head == h, v, 0.0).astype(BF16)


def _mem_kv(mem, g, w):
    batch, n_mem, d = mem.shape
    depth = w.shape[0]
    out_spec = lambda r, c: pl.BlockSpec((None, None, CA_HEADS, r, c), lambda l, b: (l, b, 0, 0, 0))
    return pl.pallas_call(
        _mem_kv_kernel,
        grid=(depth, batch),
        in_specs=[
            pl.BlockSpec((None, n_mem, d), lambda l, b: (b, 0, 0)),
            pl.BlockSpec((None, 1, d), lambda l, b: (l, 0, 0)),
            pl.BlockSpec((None, d, 2 * CA_WIDTH), lambda l, b: (l, 0, 0)),
        ],
        out_specs=[out_spec(CA_WIDTH, n_mem), out_spec(n_mem, CA_WIDTH)],
        out_shape=[
            jax.ShapeDtypeStruct((depth, batch, CA_HEADS, CA_WIDTH, n_mem), BF16),
            jax.ShapeDtypeStruct((depth, batch, CA_HEADS, n_mem, CA_WIDTH), BF16),
        ],
        compiler_params=_compiler_params(2),
        name="mem_kv",
    )(mem, g, w)


def _pooled(u_prev_ref, u_ref, u_next_ref, ext_ref, seq):
    tm = u_ref.shape[0]
    tiles_per_seq = seq // tm
    t = pl.program_id(0) % tiles_per_seq
    u = u_ref[...]
    ext_ref[pl.ds(0, POOL_HALO), :] = jnp.where(t > 0, u_prev_ref[...], 0.0)
    ext_ref[pl.ds(POOL_HALO, tm), :] = u
    ext_ref[pl.ds(POOL_HALO + tm, POOL_HALO), :] = jnp.where(t < tiles_per_seq - 1, u_next_ref[...], 0.0)

    def window(lo, hi):
        acc = ext_ref[pl.ds(POOL_HALO + lo, tm), :]
        for d in range(lo + 1, hi):
            acc = acc + ext_ref[pl.ds(POOL_HALO + d, tm), :]
        return acc

    pos = t * tm + lax.broadcasted_iota(jnp.int32, u.shape, 0)
    group = lax.broadcasted_iota(jnp.int32, u.shape, 1) // POOL_GROUP_DIM
    total = jnp.zeros_like(u)
    count = jnp.ones_like(u)
    for gi, w in enumerate(POOL_WINDOWS):
        half = w // 2
        cnt = (jnp.minimum(pos + half, seq) - jnp.maximum(pos - half, 0)).astype(F32)
        total = jnp.where(group == gi, window(-half, half), total)
        count = jnp.where(group == gi, cnt, count)
    return total / count - u


def _merge_kernel(x_ref, u_prev_ref, u_ref, u_next_ref, o_da_ref, qca_ref, kt_ref, v_ref,
                  g_pre_ref, w_gate_ref, b_gate_ref, pool_w_ref, pool_scale_ref,
                  w_pool_ref, w_da_ref, w_ca_ref, w_out_ref, g_post_ref,
                  out_ref, ext_ref, *, seq):
    x = x_ref[...]
    d = x.shape[-1]
    h = _rms_norm(x, g_pre_ref[...]).astype(BF16)
    gates = jax.nn.sigmoid(_dot(h, w_gate_ref[...]) + b_gate_ref[...])

    pooled = _pooled(u_prev_ref, u_ref, u_next_ref, ext_ref, seq).astype(BF16)
    y_pool = (_dot(pooled, pool_w_ref[...]) * pool_scale_ref[...]).astype(BF16)

    qca = qca_ref[...]
    y_ca = jnp.zeros(qca.shape, F32)
    for hd in range(CA_HEADS):
        s = _dot(qca, kt_ref[hd])
        p = jnp.exp2(s - jnp.max(s, axis=-1, keepdims=True))
        p = p * (1.0 / jnp.sum(p, axis=-1, keepdims=True))
        y_ca = y_ca + _dot(p.astype(BF16), v_ref[hd])

    merged = (gates[:, :d] * _dot(y_pool, w_pool_ref[...])
              + gates[:, d:2 * d] * _dot(o_da_ref[...], w_da_ref[...])
              + gates[:, 2 * d:] * _dot(y_ca.astype(BF16), w_ca_ref[...]))
    y = _dot(merged.astype(BF16), w_out_ref[...])
    out_ref[...] = x + _rms_norm(y, g_post_ref[...])


def _merge(x, u, o_da, qca, kt, v, g_pre, w_gate, b_gate, pool_w, pool_scale,
           w_pool, w_da, w_ca, w_out, g_post, layer, seq):
    n, d = x.shape
    n_mem = v.shape[-2]
    tiles_per_seq = seq // TOKEN_TILE
    halo_blocks = TOKEN_TILE // POOL_HALO
    n_halo = n // POOL_HALO
    row = lambda width: pl.BlockSpec((TOKEN_TILE, width), lambda i: (i, 0))
    layer_block = lambda *shape: _resident((None,) + shape, lambda i: (layer,) + (0,) * len(shape))
    mem_block = lambda r, c: pl.BlockSpec((None, None, CA_HEADS, r, c),
                                          lambda i: (layer, i // tiles_per_seq, 0, 0, 0))
    return pl.pallas_call(
        functools.partial(_merge_kernel, seq=seq),
        grid=(n // TOKEN_TILE,),
        in_specs=[
            row(d),
            pl.BlockSpec((POOL_HALO, POOL_WIDTH), lambda i: (jnp.maximum(i * halo_blocks - 1, 0), 0)),
            row(POOL_WIDTH),
            pl.BlockSpec((POOL_HALO, POOL_WIDTH), lambda i: (jnp.minimum((i + 1) * halo_blocks, n_halo - 1), 0)),
            row(DA_WIDTH),
            row(CA_WIDTH),
            mem_block(CA_WIDTH, n_mem),
            mem_block(n_mem, CA_WIDTH),
            layer_block(1, d),
            layer_block(d, 3 * d),
            layer_block(1, 3 * d),
            layer_block(POOL_WIDTH, POOL_WIDTH),
            layer_block(1, POOL_WIDTH),
            layer_block(POOL_WIDTH, d),
            layer_block(DA_WIDTH, d),
            layer_block(CA_WIDTH, d),
            layer_block(d, d),
            layer_block(1, d),
        ],
        out_specs=row(d),
        out_shape=jax.ShapeDtypeStruct((n, d), F32),
        scratch_shapes=[pltpu.VMEM((TOKEN_TILE + 2 * POOL_HALO, POOL_WIDTH), F32)],
        compiler_params=_compiler_params(1),
        name="merge",
    )(x, u, u, u, o_da, qca, kt, v, g_pre, w_gate, b_gate, pool_w, pool_scale,
      w_pool, w_da, w_ca, w_out, g_post)


def _block_diag(w):
    depth, groups, c, _ = w.shape
    eye = jnp.eye(groups, dtype=w.dtype)
    return jnp.einsum("lgcd,gh->lgchd", w, eye).reshape(depth, groups * c, groups * c)


def kernel(x, mem, positions, ffn1_pre_g, ffn1_w_up, ffn1_w_down, ffn1_post_g, mix_pre_g, w_in, pool_w, pool_scale, da_lambda_q1, da_lambda_k1, da_lambda_q2, da_lambda_k2, da_subln_g, mem_norm_g, w_mem_kv, w_gate, b_gate, w_br_pool, w_br_da, w_br_ca, w_out, mix_post_g, ffn2_pre_g, ffn2_w_up, ffn2_w_down, ffn2_post_g):
    batch, seq, d = x.shape
    depth = w_in.shape[0]
    assert seq % TOKEN_TILE == 0 and seq % ATTN_Q_TILE == 0 and seq % ATTN_K_TILE == 0
    assert w_in.shape[-1] == POOL_WIDTH + 3 * DA_WIDTH + CA_WIDTH

    vec = lambda a: a.astype(F32).reshape(depth, 1, a.shape[-1])
    bf = lambda a: a.astype(BF16)

    cos_t, sin_t = _rope_tables(positions)
    kt_all, v_all = _mem_kv(mem, vec(mem_norm_g), bf(w_mem_kv))

    ffn1 = (vec(ffn1_pre_g), bf(ffn1_w_up), bf(ffn1_w_down), vec(ffn1_post_g))
    ffn2 = (vec(ffn2_pre_g), bf(ffn2_w_up), bf(ffn2_w_down), vec(ffn2_post_g))
    mix_g, w_in_b = vec(mix_pre_g), bf(w_in)
    lam_vecs = tuple(vec(a) for a in (da_lambda_q1, da_lambda_k1, da_lambda_q2, da_lambda_k2))
    merge_params = (mix_g, bf(w_gate), vec(b_gate), bf(_block_diag(pool_w)), vec(pool_scale),
                    bf(w_br_pool), bf(w_br_da), bf(w_br_ca), bf(w_out), vec(mix_post_g))
    subln_g = vec(da_subln_g)

    xt = x.reshape(batch * seq, d)
    for layer in range(depth):
        lam_init = 0.8 - 0.6 * math.exp(-0.3 * layer)
        xt = _ffn(xt, *ffn1, layer)
        u, qt, k, vt, qca = _inproj(xt, mix_g, w_in_b, cos_t, sin_t, layer, batch, seq)
        o_da = _attention(qt, k, vt, *lam_vecs, subln_g, layer, lam_init)
        xt = _merge(xt, u, o_da, qca, kt_all, v_all, *merge_params, layer, seq)
        xt = _ffn(xt, *ffn2, layer)
    return xt.reshape(batch, seq, d)
```

```python
import functools
import math

import jax
import jax.numpy as jnp
import numpy as np
from jax import lax
from jax.experimental import pallas as pl
from jax.experimental.pallas import tpu as pltpu

F32 = jnp.float32
BF16 = jnp.bfloat16

POOL_WINDOWS = (2, 4, 8, 16)
POOL_WIDTH = 256
POOL_GROUP_DIM = POOL_WIDTH // len(POOL_WINDOWS)
POOL_HALO = max(POOL_WINDOWS) // 2
DA_HEADS = 4
DA_HEAD_DIM = 64
DA_V_DIM = 2 * DA_HEAD_DIM
DA_WIDTH = DA_HEADS * DA_V_DIM
CA_HEADS = 4
CA_HEAD_DIM = 64
CA_WIDTH = CA_HEADS * CA_HEAD_DIM
ROPE_THETA = 500000.0
ROPE_DIM = DA_HEAD_DIM // 4
ROPE_HALF = ROPE_DIM // 2
NORM_EPS = 1e-6
LOG2E = math.log2(math.e)

V7X_LANES = 128
V7X_SUBLANES = 8
V7X_VMEM_LIMIT_BYTES = 60000 * 1024

TOKEN_TILE = 512
ATTN_Q_TILE = 512
ATTN_K_TILE = 1024
BF16_SUBLANE_TILE = 2 * V7X_SUBLANES
VT_ROWS = DA_V_DIM + BF16_SUBLANE_TILE


def _compiler_params(n_grid_axes):
    return pltpu.CompilerParams(
        dimension_semantics=("arbitrary",) * n_grid_axes,
        vmem_limit_bytes=V7X_VMEM_LIMIT_BYTES,
    )


def _resident(block_shape, index_map):
    return pl.BlockSpec(block_shape, index_map, pipeline_mode=pl.Buffered(1))


def _rms_norm(x, g):
    ms = jnp.mean(x * x, axis=-1, keepdims=True)
    return x * lax.rsqrt(ms + NORM_EPS) * g


def _dot(a, b):
    return jnp.dot(a, b, preferred_element_type=F32)


def _rope_table_kernel(pos_ref, cos_ref, sin_ref):
    pos = pos_ref[...].astype(F32)
    lane = lax.broadcasted_iota(jnp.int32, pos.shape, 1)
    in_comp = lane % DA_HEAD_DIM
    freq_idx = lane % ROPE_HALF
    inv_freq = (np.float32(ROPE_THETA) ** (-np.arange(0, ROPE_DIM, 2, dtype=np.float32) / np.float32(ROPE_DIM)))
    inv = jnp.full(pos.shape, float(inv_freq[0]), F32)
    for j in range(1, ROPE_HALF):
        inv = jnp.where(freq_idx == j, float(inv_freq[j]), inv)
    ang = pos * inv
    rotary = in_comp < ROPE_DIM
    cos_ref[...] = jnp.where(rotary, jnp.cos(ang), 1.0)
    sin_v = jnp.sin(ang)
    sin_ref[...] = jnp.where(rotary, jnp.where(in_comp < ROPE_HALF, -sin_v, sin_v), 0.0)


def _rope_tables(positions):
    n = positions.size
    pos = jnp.broadcast_to(positions.reshape(n, 1), (n, V7X_LANES))
    spec = pl.BlockSpec((TOKEN_TILE, V7X_LANES), lambda i: (i, 0))
    return pl.pallas_call(
        _rope_table_kernel,
        grid=(n // TOKEN_TILE,),
        in_specs=[spec],
        out_specs=[spec, spec],
        out_shape=[jax.ShapeDtypeStruct((n, V7X_LANES), F32)] * 2,
        compiler_params=_compiler_params(1),
        name="rope_tables",
    )(pos)


def _ffn_kernel(x_ref, g_pre_ref, w_up_ref, w_down_ref, g_post_ref, o_ref, *, d_ff):
    x = x_ref[...]
    h = _rms_norm(x, g_pre_ref[...]).astype(BF16)
    ab = _dot(h, w_up_ref[...])
    a = ab[:, :d_ff]
    b = ab[:, d_ff:]
    act = (a * jax.nn.sigmoid(a) * b).astype(BF16)
    y = _dot(act, w_down_ref[...])
    o_ref[...] = x + 0.5 * _rms_norm(y, g_post_ref[...])


def _ffn(x, g_pre, w_up, w_down, g_post, layer):
    n, d = x.shape
    d_ff = w_down.shape[1]
    row = pl.BlockSpec((TOKEN_TILE, d), lambda i: (i, 0))
    gain = _resident((None, 1, d), lambda i: (layer, 0, 0))
    return pl.pallas_call(
        functools.partial(_ffn_kernel, d_ff=d_ff),
        grid=(n // TOKEN_TILE,),
        in_specs=[
            row,
            gain,
            _resident((None, d, 2 * d_ff), lambda i: (layer, 0, 0)),
            _resident((None, d_ff, d), lambda i: (layer, 0, 0)),
            gain,
        ],
        out_specs=row,
        out_shape=jax.ShapeDtypeStruct((n, d), F32),
        compiler_params=_compiler_params(1),
        name="ffn",
    )(x, g_pre, w_up, w_down, g_post)


def _apply_rope(t, cos_t, sin_t):
    width = t.shape[-1]
    lane = lax.broadcasted_iota(jnp.int32, t.shape, 1) % DA_HEAD_DIM
    partner = jnp.where(lane < ROPE_HALF,
                        pltpu.roll(t, width - ROPE_HALF, 1),
                        pltpu.roll(t, ROPE_HALF, 1))
    reps = width // cos_t.shape[-1]
    cos_w = jnp.concatenate([cos_t] * reps, axis=-1)
    sin_w = jnp.concatenate([sin_t] * reps, axis=-1)
    return t * cos_w + partner * sin_w


def _inproj_kernel(x_ref, g_ref, w_ref, cos_ref, sin_ref,
                   u_ref, qt_ref, k_ref, vt_ref, qca_ref):
    h = _rms_norm(x_ref[...], g_ref[...]).astype(BF16)
    proj = _dot(h, w_ref[...])
    o_q = POOL_WIDTH
    o_k = o_q + DA_WIDTH
    o_v = o_k + DA_WIDTH
    o_ca = o_v + DA_WIDTH
    cos_t = cos_ref[...]
    sin_t = sin_ref[...]
    u_ref[...] = proj[:, :o_q]
    q = _apply_rope(proj[:, o_q:o_k], cos_t, sin_t) * (DA_HEAD_DIM ** -0.5 * LOG2E)
    qt_ref[...] = q.T.astype(BF16)
    k_ref[...] = _apply_rope(proj[:, o_k:o_v], cos_t, sin_t).astype(BF16)
    vt = proj[:, o_v:o_ca].T.astype(BF16)
    ones_row = (lax.broadcasted_iota(jnp.int32, (BF16_SUBLANE_TILE, vt.shape[1]), 0) == 0).astype(BF16)
    for hd in range(DA_HEADS):
        vt_ref[hd, :DA_V_DIM, :] = vt[hd * DA_V_DIM:(hd + 1) * DA_V_DIM]
        vt_ref[hd, DA_V_DIM:, :] = ones_row
    qca_ref[...] = (proj[:, o_ca:] * (CA_HEAD_DIM ** -0.5 * LOG2E)).astype(BF16)


def _inproj(x, g, w_in, cos_t, sin_t, layer, batch, seq):
    n, d = x.shape
    tiles_per_seq = seq // TOKEN_TILE
    row = lambda width: pl.BlockSpec((TOKEN_TILE, width), lambda i: (i, 0))
    col = pl.BlockSpec((None, DA_WIDTH, TOKEN_TILE), lambda i: (i // tiles_per_seq, 0, i % tiles_per_seq))
    vt_col = pl.BlockSpec((None, DA_HEADS, VT_ROWS, TOKEN_TILE),
                          lambda i: (i // tiles_per_seq, 0, 0, i % tiles_per_seq))
    return pl.pallas_call(
        _inproj_kernel,
        grid=(n // TOKEN_TILE,),
        in_specs=[
            row(d),
            _resident((None, 1, d), lambda i: (layer, 0, 0)),
            _resident((None, d, w_in.shape[-1]), lambda i: (layer, 0, 0)),
            row(V7X_LANES),
            row(V7X_LANES),
        ],
        out_specs=[row(POOL_WIDTH), col, row(DA_WIDTH), vt_col, row(CA_WIDTH)],
        out_shape=[
            jax.ShapeDtypeStruct((n, POOL_WIDTH), F32),
            jax.ShapeDtypeStruct((batch, DA_WIDTH, seq), BF16),
            jax.ShapeDtypeStruct((n, DA_WIDTH), BF16),
            jax.ShapeDtypeStruct((batch, DA_HEADS, VT_ROWS, seq), BF16),
            jax.ShapeDtypeStruct((n, CA_WIDTH), BF16),
        ],
        compiler_params=_compiler_params(1),
        name="inproj",
    )(x, g, w_in, cos_t, sin_t)


def _attn_kernel(qt_ref, k_ref, vt_ref, lq1_ref, lk1_ref, lq2_ref, lk2_ref, g_ref, o_ref,
                 s_even, s_odd, acc_ref, *, lam_init, seq, tk):
    qt = qt_ref[...]
    tq = qt.shape[1]
    n_blk = seq // tk
    comp_row = lax.broadcasted_iota(jnp.int32, qt.shape, 0) < DA_HEAD_DIM
    zero = jnp.zeros_like(qt)
    q_comp = (jnp.where(comp_row, qt, zero), jnp.where(comp_row, zero, qt))

    def scores(blk, s_buf):
        start = pl.multiple_of(blk * tk, tk)
        k_blk = k_ref[pl.ds(start, tk), :]
        blk_max = []
        for c in range(2):
            s = _dot(k_blk, q_comp[c])
            s_buf[c] = s
            blk_max.append(jnp.max(s, axis=0, keepdims=True))
        return tuple(blk_max)

    def accumulate(blk, s_buf, blk_max, m):
        start = pl.multiple_of(blk * tk, tk)
        vt_blk = vt_ref[:, pl.ds(start, tk)]
        m_out = []
        for c in range(2):
            m_new = jnp.maximum(m[c], blk_max[c])
            p = jnp.exp2(s_buf[c] - m_new).astype(BF16)
            alpha = jnp.exp2(m[c] - m_new)
            acc_ref[c] = alpha * acc_ref[c] + _dot(vt_blk, p)
            m_out.append(m_new)
        return tuple(m_out)

    acc_ref[...] = jnp.zeros(acc_ref.shape, F32)
    neg_inf = jnp.full((1, tq), -jnp.inf, F32)
    max_even = scores(0, s_even)

    def pair(i, carry):
        m, max_even = carry
        max_odd = scores(2 * i + 1, s_odd)
        m = accumulate(2 * i, s_even, max_even, m)
        max_even = scores(2 * i + 2, s_even)
        m = accumulate(2 * i + 1, s_odd, max_odd, m)
        return m, max_even

    m, max_even = lax.fori_loop(0, n_blk // 2 - 1, pair, ((neg_inf, neg_inf), max_even))
    max_odd = scores(n_blk - 1, s_odd)
    m = accumulate(n_blk - 2, s_even, max_even, m)
    accumulate(n_blk - 1, s_odd, max_odd, m)

    lam = (jnp.exp(jnp.sum(lq1_ref[...] * lk1_ref[...], axis=-1, keepdims=True))
           - jnp.exp(jnp.sum(lq2_ref[...] * lk2_ref[...], axis=-1, keepdims=True))
           + lam_init)
    acc0 = acc_ref[0]
    acc1 = acc_ref[1]
    o_t = (acc0[:DA_V_DIM] * (1.0 / acc0[DA_V_DIM:DA_V_DIM + 1])
           - lam * (acc1[:DA_V_DIM] * (1.0 / acc1[DA_V_DIM:DA_V_DIM + 1])))
    o = _rms_norm(o_t.T, g_ref[...]) * (1.0 - lam_init)
    o_ref[...] = o.astype(BF16)


def _attention(qt, k, vt, lq1, lk1, lq2, lk2, g, layer, lam_init):
    batch, _, seq = qt.shape
    k = k.reshape(batch, seq, DA_WIDTH)
    tk = min(ATTN_K_TILE, seq // 2)
    assert seq % (2 * tk) == 0
    small = lambda width: _resident((None, 1, width), lambda b, h, i: (layer, 0, 0))
    out = pl.pallas_call(
        functools.partial(_attn_kernel, lam_init=lam_init, seq=seq, tk=tk),
        grid=(batch, DA_HEADS, seq // ATTN_Q_TILE),
        in_specs=[
            pl.BlockSpec((None, DA_V_DIM, ATTN_Q_TILE), lambda b, h, i: (b, h, i)),
            pl.BlockSpec((None, seq, DA_V_DIM), lambda b, h, i: (b, 0, h)),
            pl.BlockSpec((None, None, VT_ROWS, seq), lambda b, h, i: (b, h, 0, 0)),
            small(DA_HEAD_DIM), small(DA_HEAD_DIM), small(DA_HEAD_DIM), small(DA_HEAD_DIM),
            small(DA_V_DIM),
        ],
        out_specs=pl.BlockSpec((None, ATTN_Q_TILE, DA_V_DIM), lambda b, h, i: (b, i, h)),
        out_shape=jax.ShapeDtypeStruct((batch, seq, DA_WIDTH), BF16),
        scratch_shapes=[
            pltpu.VMEM((2, tk, ATTN_Q_TILE), F32),
            pltpu.VMEM((2, tk, ATTN_Q_TILE), F32),
            pltpu.VMEM((2, VT_ROWS, ATTN_Q_TILE), F32),
        ],
        compiler_params=_compiler_params(3),
        name="diff_attention",
    )(qt, k, vt, lq1, lk1, lq2, lk2, g)
    return out.reshape(batch * seq, DA_WIDTH)


def _mem_kv_kernel(mem_ref, g_ref, w_ref, kt_ref, v_ref):
    hm = _rms_norm(mem_ref[...], g_ref[...]).astype(BF16)
    kv = _dot(hm, w_ref[...])
    k_t = kv[:, :CA_WIDTH].T
    v = kv[:, CA_WIDTH:]
    k_head = lax.broadcasted_iota(jnp.int32, k_t.shape, 0) // CA_HEAD_DIM
    v_head = lax.broadcasted_iota(jnp.int32, v.shape, 1) // CA_HEAD_DIM
    for h in range(CA_HEADS):
        kt_ref[h] = jnp.where(k_head == h, k_t, 0.0).astype(BF16)
        v_ref[h] = jnp.where(v_head == h, v, 0.0).astype(BF16)


def _mem_kv(mem, g, w):
    batch, n_mem, d = mem.shape
    depth = w.shape[0]
    out_spec = lambda r, c: pl.BlockSpec((None, None, CA_HEADS, r, c), lambda l, b: (l, b, 0, 0, 0))
    return pl.pallas_call(
        _mem_kv_kernel,
        grid=(depth, batch),
        in_specs=[
            pl.BlockSpec((None, n_mem, d), lambda l, b: (b, 0, 0)),
            pl.BlockSpec((None, 1, d), lambda l, b: (l, 0, 0)),
            pl.BlockSpec((None, d, 2 * CA_WIDTH), lambda l, b: (l, 0, 0)),
        ],
        out_specs=[out_spec(CA_WIDTH, n_mem), out_spec(n_mem, CA_WIDTH)],
        out_shape=[
            jax.ShapeDtypeStruct((depth, batch, CA_HEADS, CA_WIDTH, n_mem), BF16),
            jax.ShapeDtypeStruct((depth, batch, CA_HEADS, n_mem, CA_WIDTH), BF16),
        ],
        compiler_params=_compiler_params(2),
        name="mem_kv",
    )(mem, g, w)


def _pooled(u_prev_ref, u_ref, u_next_ref, ext_ref, seq):
    tm = u_ref.shape[0]
    tiles_per_seq = seq // tm
    t = pl.program_id(0) % tiles_per_seq
    u = u_ref[...]
    ext_ref[pl.ds(0, POOL_HALO), :] = jnp.where(t > 0, u_prev_ref[...], 0.0)
    ext_ref[pl.ds(POOL_HALO, tm), :] = u
    ext_ref[pl.ds(POOL_HALO + tm, POOL_HALO), :] = jnp.where(t < tiles_per_seq - 1, u_next_ref[...], 0.0)

    def window(lo, hi):
        acc = ext_ref[pl.ds(POOL_HALO + lo, tm), :]
        for d in range(lo + 1, hi):
            acc = acc + ext_ref[pl.ds(POOL_HALO + d, tm), :]
        return acc

    pos = t * tm + lax.broadcasted_iota(jnp.int32, u.shape, 0)
    group = lax.broadcasted_iota(jnp.int32, u.shape, 1) // POOL_GROUP_DIM
    total = jnp.zeros_like(u)
    count = jnp.ones_like(u)
    for gi, w in enumerate(POOL_WINDOWS):
        half = w // 2
        cnt = (jnp.minimum(pos + half, seq) - jnp.maximum(pos - half, 0)).astype(F32)
        total = jnp.where(group == gi, window(-half, half), total)
        count = jnp.where(group == gi, cnt, count)
    return total / count - u


def _merge_kernel(x_ref, u_prev_ref, u_ref, u_next_ref, o_da_ref, qca_ref, kt_ref, v_ref,
                  g_pre_ref, w_gate_ref, b_gate_ref, pool_w_ref, pool_scale_ref,
                  w_pool_ref, w_da_ref, w_ca_ref, w_out_ref, g_post_ref,
                  out_ref, ext_ref, *, seq):
    x = x_ref[...]
    d = x.shape[-1]
    h = _rms_norm(x, g_pre_ref[...]).astype(BF16)
    gates = jax.nn.sigmoid(_dot(h, w_gate_ref[...]) + b_gate_ref[...])

    pooled = _pooled(u_prev_ref, u_ref, u_next_ref, ext_ref, seq).astype(BF16)
    y_pool = (_dot(pooled, pool_w_ref[...]) * pool_scale_ref[...]).astype(BF16)

    qca = qca_ref[...]
    y_ca = jnp.zeros(qca.shape, F32)
    for hd in range(CA_HEADS):
        s = _dot(qca, kt_ref[hd])
        p = jnp.exp2(s - jnp.max(s, axis=-1, keepdims=True))
        p = p * (1.0 / jnp.sum(p, axis=-1, keepdims=True))
        y_ca = y_ca + _dot(p.astype(BF16), v_ref[hd])

    merged = (gates[:, :d] * _dot(y_pool, w_pool_ref[...])
              + gates[:, d:2 * d] * _dot(o_da_ref[...], w_da_ref[...])
              + gates[:, 2 * d:] * _dot(y_ca.astype(BF16), w_ca_ref[...]))
    y = _dot(merged.astype(BF16), w_out_ref[...])
    out_ref[...] = x + _rms_norm(y, g_post_ref[...])


def _merge(x, u, o_da, qca, kt, v, g_pre, w_gate, b_gate, pool_w, pool_scale,
           w_pool, w_da, w_ca, w_out, g_post, layer, seq):
    n, d = x.shape
    n_mem = v.shape[-2]
    tiles_per_seq = seq // TOKEN_TILE
    halo_blocks = TOKEN_TILE // POOL_HALO
    n_halo = n // POOL_HALO
    row = lambda width: pl.BlockSpec((TOKEN_TILE, width), lambda i: (i, 0))
    layer_block = lambda *shape: _resident((None,) + shape, lambda i: (layer,) + (0,) * len(shape))
    mem_block = lambda r, c: pl.BlockSpec((None, None, CA_HEADS, r, c),
                                          lambda i: (layer, i // tiles_per_seq, 0, 0, 0))
    return pl.pallas_call(
        functools.partial(_merge_kernel, seq=seq),
        grid=(n // TOKEN_TILE,),
        in_specs=[
            row(d),
            pl.BlockSpec((POOL_HALO, POOL_WIDTH), lambda i: (jnp.maximum(i * halo_blocks - 1, 0), 0)),
            row(POOL_WIDTH),
            pl.BlockSpec((POOL_HALO, POOL_WIDTH), lambda i: (jnp.minimum((i + 1) * halo_blocks, n_halo - 1), 0)),
            row(DA_WIDTH),
            row(CA_WIDTH),
            mem_block(CA_WIDTH, n_mem),
            mem_block(n_mem, CA_WIDTH),
            layer_block(1, d),
            layer_block(d, 3 * d),
            layer_block(1, 3 * d),
            layer_block(POOL_WIDTH, POOL_WIDTH),
            layer_block(1, POOL_WIDTH),
            layer_block(POOL_WIDTH, d),
            layer_block(DA_WIDTH, d),
            layer_block(CA_WIDTH, d),
            layer_block(d, d),
            layer_block(1, d),
        ],
        out_specs=row(d),
        out_shape=jax.ShapeDtypeStruct((n, d), F32),
        scratch_shapes=[pltpu.VMEM((TOKEN_TILE + 2 * POOL_HALO, POOL_WIDTH), F32)],
        compiler_params=_compiler_params(1),
        name="merge",
    )(x, u, u, u, o_da, qca, kt, v, g_pre, w_gate, b_gate, pool_w, pool_scale,
      w_pool, w_da, w_ca, w_out, g_post)


def _block_diag(w):
    depth, groups, c, _ = w.shape
    eye = jnp.eye(groups, dtype=w.dtype)
    return jnp.einsum("lgcd,gh->lgchd", w, eye).reshape(depth, groups * c, groups * c)


def kernel(x, mem, positions, ffn1_pre_g, ffn1_w_up, ffn1_w_down, ffn1_post_g, mix_pre_g, w_in, pool_w, pool_scale, da_lambda_q1, da_lambda_k1, da_lambda_q2, da_lambda_k2, da_subln_g, mem_norm_g, w_mem_kv, w_gate, b_gate, w_br_pool, w_br_da, w_br_ca, w_out, mix_post_g, ffn2_pre_g, ffn2_w_up, ffn2_w_down, ffn2_post_g):
    batch, seq, d = x.shape
    depth = w_in.shape[0]
    assert seq % TOKEN_TILE == 0 and seq % ATTN_Q_TILE == 0
    assert w_in.shape[-1] == POOL_WIDTH + 3 * DA_WIDTH + CA_WIDTH

    vec = lambda a: a.astype(F32).reshape(depth, 1, a.shape[-1])
    bf = lambda a: a.astype(BF16)

    cos_t, sin_t = _rope_tables(positions)
    kt_all, v_all = _mem_kv(mem, vec(mem_norm_g), bf(w_mem_kv))

    ffn1 = (vec(ffn1_pre_g), bf(ffn1_w_up), bf(ffn1_w_down), vec(ffn1_post_g))
    ffn2 = (vec(ffn2_pre_g), bf(ffn2_w_up), bf(ffn2_w_down), vec(ffn2_post_g))
    mix_g, w_in_b = vec(mix_pre_g), bf(w_in)
    lam_vecs = tuple(vec(a) for a in (da_lambda_q1, da_lambda_k1, da_lambda_q2, da_lambda_k2))
    merge_params = (mix_g, bf(w_gate), vec(b_gate), bf(_block_diag(pool_w)), vec(pool_scale),
                    bf(w_br_pool), bf(w_br_da), bf(w_br_ca), bf(w_out), vec(mix_post_g))
    subln_g = vec(da_subln_g)

    xt = x.reshape(batch * seq, d)
    for layer in range(depth):
        lam_init = 0.8 - 0.6 * math.exp(-0.3 * layer)
        xt = _ffn(xt, *ffn1, layer)
        u, qt, k, vt, qca = _inproj(xt, mix_g, w_in_b, cos_t, sin_t, layer, batch, seq)
        o_da = _attention(qt, k, vt, *lam_vecs, subln_g, layer, lam_init)
        xt = _merge(xt, u, o_da, qca, kt_all, v_all, *merge_params, layer, seq)
        xt = _ffn(xt, *ffn2, layer)
    return xt.reshape(batch, seq, d)
```

```python
import functools
import math

import jax
import jax.numpy as jnp
import numpy as np
from jax import lax
from jax.experimental import pallas as pl
from jax.experimental.pallas import tpu as pltpu

F32 = jnp.float32
BF16 = jnp.bfloat16
F8 = jnp.float8_e4m3fn

POOL_WINDOWS = (2, 4, 8, 16)
POOL_WIDTH = 256
POOL_GROUP_DIM = POOL_WIDTH // len(POOL_WINDOWS)
POOL_HALO = max(POOL_WINDOWS) // 2
DA_HEADS = 4
DA_HEAD_DIM = 64
DA_V_DIM = 2 * DA_HEAD_DIM
DA_WIDTH = DA_HEADS * DA_V_DIM
CA_HEADS = 4
CA_HEAD_DIM = 64
CA_WIDTH = CA_HEADS * CA_HEAD_DIM
ROPE_THETA = 500000.0
ROPE_DIM = DA_HEAD_DIM // 4
ROPE_HALF = ROPE_DIM // 2
NORM_EPS = 1e-6
LOG2E = math.log2(math.e)

V7X_LANES = 128
V7X_SUBLANES = 8
V7X_VMEM_LIMIT_BYTES = 60000 * 1024

TOKEN_TILE = 512
ATTN_Q_TILE = 512
ATTN_K_TILE = 2048
ATTN_ROW_CHUNK = 1024
BF16_SUBLANE_TILE = 2 * V7X_SUBLANES
VT_ROWS = DA_V_DIM + BF16_SUBLANE_TILE
V7X_MXU_DEPTH = 256
QK_STACK = V7X_MXU_DEPTH


def _compiler_params(n_grid_axes):
    return pltpu.CompilerParams(
        dimension_semantics=("arbitrary",) * n_grid_axes,
        vmem_limit_bytes=V7X_VMEM_LIMIT_BYTES,
    )


def _resident(block_shape, index_map):
    return pl.BlockSpec(block_shape, index_map, pipeline_mode=pl.Buffered(1))


def _rms_norm(x, g):
    ms = jnp.mean(x * x, axis=-1, keepdims=True)
    return x * lax.rsqrt(ms + NORM_EPS) * g


def _dot(a, b):
    return jnp.dot(a, b, preferred_element_type=F32)


def _rope_table_kernel(pos_ref, cos_ref, sin_ref):
    pos = pos_ref[...].astype(F32)
    lane = lax.broadcasted_iota(jnp.int32, pos.shape, 1)
    in_comp = lane % DA_HEAD_DIM
    freq_idx = lane % ROPE_HALF
    inv_freq = (np.float32(ROPE_THETA) ** (-np.arange(0, ROPE_DIM, 2, dtype=np.float32) / np.float32(ROPE_DIM)))
    inv = jnp.full(pos.shape, float(inv_freq[0]), F32)
    for j in range(1, ROPE_HALF):
        inv = jnp.where(freq_idx == j, float(inv_freq[j]), inv)
    ang = pos * inv
    rotary = in_comp < ROPE_DIM
    cos_ref[...] = jnp.where(rotary, jnp.cos(ang), 1.0)
    sin_v = jnp.sin(ang)
    sin_ref[...] = jnp.where(rotary, jnp.where(in_comp < ROPE_HALF, -sin_v, sin_v), 0.0)


def _rope_tables(positions):
    n = positions.size
    pos = jnp.broadcast_to(positions.reshape(n, 1), (n, V7X_LANES))
    spec = pl.BlockSpec((TOKEN_TILE, V7X_LANES), lambda i: (i, 0))
    return pl.pallas_call(
        _rope_table_kernel,
        grid=(n // TOKEN_TILE,),
        in_specs=[spec],
        out_specs=[spec, spec],
        out_shape=[jax.ShapeDtypeStruct((n, V7X_LANES), F32)] * 2,
        compiler_params=_compiler_params(1),
        name="rope_tables",
    )(pos)


def _ffn_kernel(x_ref, g_pre_ref, w_up_ref, w_down_ref, g_post_ref, o_ref, *, d_ff):
    x = x_ref[...]
    h = _rms_norm(x, g_pre_ref[...]).astype(BF16)
    ab = _dot(h, w_up_ref[...])
    a = ab[:, :d_ff]
    b = ab[:, d_ff:]
    act = (a * jax.nn.sigmoid(a) * b).astype(BF16)
    y = _dot(act, w_down_ref[...])
    o_ref[...] = x + 0.5 * _rms_norm(y, g_post_ref[...])


def _ffn(x, g_pre, w_up, w_down, g_post, layer):
    n, d = x.shape
    d_ff = w_down.shape[1]
    row = pl.BlockSpec((TOKEN_TILE, d), lambda i: (i, 0))
    gain = _resident((None, 1, d), lambda i: (layer, 0, 0))
    return pl.pallas_call(
        functools.partial(_ffn_kernel, d_ff=d_ff),
        grid=(n // TOKEN_TILE,),
        in_specs=[
            row,
            gain,
            _resident((None, d, 2 * d_ff), lambda i: (layer, 0, 0)),
            _resident((None, d_ff, d), lambda i: (layer, 0, 0)),
            gain,
        ],
        out_specs=row,
        out_shape=jax.ShapeDtypeStruct((n, d), F32),
        compiler_params=_compiler_params(1),
        name="ffn",
    )(x, g_pre, w_up, w_down, g_post)


def _apply_rope(t, cos_t, sin_t):
    width = t.shape[-1]
    lane = lax.broadcasted_iota(jnp.int32, t.shape, 1) % DA_HEAD_DIM
    partner = jnp.where(lane < ROPE_HALF,
                        pltpu.roll(t, width - ROPE_HALF, 1),
                        pltpu.roll(t, ROPE_HALF, 1))
    reps = width // cos_t.shape[-1]
    cos_w = jnp.concatenate([cos_t] * reps, axis=-1)
    sin_w = jnp.concatenate([sin_t] * reps, axis=-1)
    return t * cos_w + partner * sin_w


def _split_fp8(x):
    fp8_max = float(jnp.finfo(F8).max)
    x = jnp.clip(x, -fp8_max, fp8_max)
    hi = x.astype(F8)
    return hi, x - hi.astype(F32)


def _inproj_kernel(x_ref, g_ref, w_ref, cos_ref, sin_ref,
                   u_ref, q8_ref, k8_ref, vt_ref, qca_ref):
    h = _rms_norm(x_ref[...], g_ref[...]).astype(BF16)
    proj = _dot(h, w_ref[...])
    tm = proj.shape[0]
    o_q = POOL_WIDTH
    o_k = o_q + DA_WIDTH
    o_v = o_k + DA_WIDTH
    o_ca = o_v + DA_WIDTH
    cos_t = cos_ref[...]
    sin_t = sin_ref[...]
    u_ref[...] = proj[:, :o_q]
    qk_scale = math.sqrt(DA_HEAD_DIM ** -0.5 * LOG2E)
    q_t = (_apply_rope(proj[:, o_q:o_k], cos_t, sin_t) * qk_scale).T
    k = _apply_rope(proj[:, o_k:o_v], cos_t, sin_t) * qk_scale

    first_half = lax.broadcasted_iota(jnp.int32, (tm, DA_V_DIM), 1) < DA_HEAD_DIM
    zero_rows = jnp.zeros((DA_HEAD_DIM, tm), F8)
    for hd in range(DA_HEADS):
        k_head = k[:, hd * DA_V_DIM:(hd + 1) * DA_V_DIM]
        k_swap = pltpu.roll(k_head, DA_HEAD_DIM, 1)
        for c in range(2):
            both = jnp.where(first_half, k_head, k_swap) if c == 0 else jnp.where(first_half, k_swap, k_head)
            hi, lo = _split_fp8(both)
            col = (2 * hd + c) * QK_STACK
            k8_ref[:, col:col + DA_V_DIM] = hi
            k8_ref[:, col + DA_V_DIM:col + QK_STACK] = jnp.where(first_half, lo, 0.0).astype(F8)
            row = hd * DA_V_DIM + c * DA_HEAD_DIM
            hi, lo = _split_fp8(q_t[row:row + DA_HEAD_DIM])
            base = c * QK_STACK
            q8_ref[hd, base:base + DA_HEAD_DIM] = hi
            q8_ref[hd, base + DA_HEAD_DIM:base + 2 * DA_HEAD_DIM] = lo.astype(F8)
            q8_ref[hd, base + 2 * DA_HEAD_DIM:base + 3 * DA_HEAD_DIM] = hi
            q8_ref[hd, base + 3 * DA_HEAD_DIM:base + QK_STACK] = zero_rows
    vt = proj[:, o_v:o_ca].T.astype(BF16)
    ones_row = (lax.broadcasted_iota(jnp.int32, (BF16_SUBLANE_TILE, vt.shape[1]), 0) == 0).astype(BF16)
    for hd in range(DA_HEADS):
        vt_ref[hd, :DA_V_DIM, :] = vt[hd * DA_V_DIM:(hd + 1) * DA_V_DIM]
        vt_ref[hd, DA_V_DIM:, :] = ones_row
    qca_ref[...] = (proj[:, o_ca:] * (CA_HEAD_DIM ** -0.5 * LOG2E)).astype(BF16)


def _inproj(x, g, w_in, cos_t, sin_t, layer, batch, seq):
    n, d = x.shape
    tiles_per_seq = seq // TOKEN_TILE
    row = lambda width: pl.BlockSpec((TOKEN_TILE, width), lambda i: (i, 0))
    col = lambda rows: pl.BlockSpec((None, DA_HEADS, rows, TOKEN_TILE),
                                    lambda i: (i // tiles_per_seq, 0, 0, i % tiles_per_seq))
    return pl.pallas_call(
        _inproj_kernel,
        grid=(n // TOKEN_TILE,),
        in_specs=[
            row(d),
            _resident((None, 1, d), lambda i: (layer, 0, 0)),
            _resident((None, d, w_in.shape[-1]), lambda i: (layer, 0, 0)),
            row(V7X_LANES),
            row(V7X_LANES),
        ],
        out_specs=[row(POOL_WIDTH), col(2 * QK_STACK), row(DA_HEADS * 2 * QK_STACK), col(VT_ROWS), row(CA_WIDTH)],
        out_shape=[
            jax.ShapeDtypeStruct((n, POOL_WIDTH), F32),
            jax.ShapeDtypeStruct((batch, DA_HEADS, 2 * QK_STACK, seq), F8),
            jax.ShapeDtypeStruct((n, DA_HEADS * 2 * QK_STACK), F8),
            jax.ShapeDtypeStruct((batch, DA_HEADS, VT_ROWS, seq), BF16),
            jax.ShapeDtypeStruct((n, CA_WIDTH), BF16),
        ],
        compiler_params=_compiler_params(1),
        name="inproj",
    )(x, g, w_in, cos_t, sin_t)


def _attn_kernel(q8_ref, k8_ref, vt_ref, lq1_ref, lk1_ref, lq2_ref, lk2_ref, g_ref, o_ref,
                 s_even, s_odd, acc_ref, *, lam_init, seq, tq, tk, chunk):
    n_blk = seq // tk
    n_steps = (seq // tq) * n_blk
    neg_inf_rows = jnp.full((V7X_SUBLANES, tq), -jnp.inf, F32)

    def score_chunk(q_start, k_start, r, s_buf, part_max):
        rows = pl.ds(pl.multiple_of(r, chunk), chunk)
        out = []
        for c in range(2):
            cols = slice(c * QK_STACK, (c + 1) * QK_STACK)
            s = _dot(k8_ref[pl.ds(pl.multiple_of(k_start + r, chunk), chunk), cols],
                     q8_ref[cols, pl.ds(q_start, tq)])
            s_buf[c, rows, :] = s
            s_max = jnp.max(s.reshape(-1, V7X_SUBLANES, tq), axis=0)
            out.append(jnp.maximum(part_max[c], s_max))
        return tuple(out)

    def softmax_chunk(k_start, r, s_buf, m_new):
        rows = pl.ds(pl.multiple_of(r, chunk), chunk)
        vt_rows = vt_ref[:, pl.ds(pl.multiple_of(k_start + r, chunk), chunk)]
        for c in range(2):
            p = jnp.exp2((s_buf[c, rows, :] - m_new[c]).astype(BF16))
            acc_ref[c] += _dot(vt_rows, p)

    def block_max(part_max):
        return tuple(jnp.max(pm, axis=0, keepdims=True) for pm in part_max)

    def step(t, t_next, s_cur, s_next, m, cur_max):
        k_blk = t % n_blk
        k_start = k_blk * tk
        next_q_start = pl.multiple_of((t_next // n_blk) * tq, tq)
        next_k_start = (t_next % n_blk) * tk
        m_new = []
        for c in range(2):
            m_prev = jnp.where(k_blk == 0, -jnp.inf, m[c])
            m_new.append(jnp.maximum(m_prev, cur_max[c]))
            acc_ref[c] = acc_ref[c] * jnp.exp2(m_prev - m_new[c])
        m_new = tuple(m_new)

        def body(ci, part_max):
            r = ci * chunk
            part_max = score_chunk(next_q_start, next_k_start, r, s_next, part_max)
            softmax_chunk(k_start, r, s_cur, m_new)
            return part_max

        part_max = lax.fori_loop(0, tk // chunk, body, (neg_inf_rows, neg_inf_rows))
        return m_new, block_max(part_max)

    lam =(jnp.exp(jnp.sum(lq1_ref[...] * lk1_ref[...], axis=-1, keepdims=True))
           - jnp.exp(jnp.sum(lq2_ref[...] * lk2_ref[...], axis=-1, keepdims=True))
           + lam_init)

    def finalize(q_tile):
        acc0 = acc_ref[0]
        acc1 = acc_ref[1]
        o_t = (acc0[:DA_V_DIM] * (1.0 / acc0[DA_V_DIM:DA_V_DIM + 1])
               - lam * (acc1[:DA_V_DIM] * (1.0 / acc1[DA_V_DIM:DA_V_DIM + 1])))
        o = _rms_norm(o_t.T, g_ref[...]) * (1.0 - lam_init)
        o_ref[pl.ds(pl.multiple_of(q_tile * tq, tq), tq), :] = o.astype(BF16)

    acc_ref[...] = jnp.zeros(acc_ref.shape, F32)
    first_max = lax.fori_loop(
        0, tk // chunk,
        lambda ci, pm: score_chunk(0, 0, ci * chunk, s_even, pm),
        (neg_inf_rows, neg_inf_rows))
    neg_inf = jnp.full((1, tq), -jnp.inf, F32)

    def pair(i, carry):
        m, max_even = carry
        t = 2 * i
        m, max_odd = step(t, t + 1, s_even, s_odd, m, max_even)
        m, max_even = step(t + 1, jnp.where(t + 2 == n_steps, 0, t + 2), s_odd, s_even, m, max_odd)

        @pl.when((t + 1) % n_blk == n_blk - 1)
        def _():
            finalize((t + 1) // n_blk)

        return m, max_even

    lax.fori_loop(0, n_steps // 2, pair, ((neg_inf, neg_inf), block_max(first_max)))


def _attention(q8, k8, vt, lq1, lk1, lq2, lk2, g, layer, lam_init):
    batch, _, _, seq = q8.shape
    k8 = k8.reshape(batch, seq, DA_HEADS * 2 * QK_STACK)
    tq = ATTN_Q_TILE
    tk = min(ATTN_K_TILE, seq // 2)
    chunk = min(ATTN_ROW_CHUNK, tk)
    assert seq % tq == 0 and seq % (2 * tk) == 0 and tk % chunk == 0
    small = lambda width: _resident((None, 1, width), lambda b, h: (layer, 0, 0))
    out = pl.pallas_call(
        functools.partial(_attn_kernel, lam_init=lam_init, seq=seq, tq=tq, tk=tk, chunk=chunk),
        grid=(batch, DA_HEADS),
        in_specs=[
            pl.BlockSpec((None, None, 2 * QK_STACK, seq), lambda b, h: (b, h, 0, 0)),
            pl.BlockSpec((None, seq, 2 * QK_STACK), lambda b, h: (b, 0, h)),
            pl.BlockSpec((None, None, VT_ROWS, seq), lambda b, h: (b, h, 0, 0)),
            small(DA_HEAD_DIM), small(DA_HEAD_DIM), small(DA_HEAD_DIM), small(DA_HEAD_DIM),
            small(DA_V_DIM),
        ],
        out_specs=pl.BlockSpec((None, seq, DA_V_DIM), lambda b, h: (b, 0, h)),
        out_shape=jax.ShapeDtypeStruct((batch, seq, DA_WIDTH), BF16),
        scratch_shapes=[
            pltpu.VMEM((2, tk, tq), F32),
            pltpu.VMEM((2, tk, tq), F32),
            pltpu.VMEM((2, VT_ROWS, tq), F32),
        ],
        compiler_params=_compiler_params(2),
        name="diff_attention",
    )(q8, k8, vt, lq1, lk1, lq2, lk2, g)
    return out.reshape(batch * seq, DA_WIDTH)


def _mem_kv_kernel(mem_ref, g_ref, w_ref, kt_ref, v_ref):
    hm = _rms_norm(mem_ref[...], g_ref[...]).astype(BF16)
    kv = _dot(hm, w_ref[...])
    k_t = kv[:, :CA_WIDTH].T
    v = kv[:, CA_WIDTH:]
    k_head = lax.broadcasted_iota(jnp.int32, k_t.shape, 0) // CA_HEAD_DIM
    v_head = lax.broadcasted_iota(jnp.int32, v.shape, 1) // CA_HEAD_DIM
    for h in range(CA_HEADS):
        kt_ref[h] = jnp.where(k_head == h, k_t, 0.0).astype(BF16)
        v_ref[h] = jnp.where(v_head == h, v, 0.0).astype(BF16)


def _mem_kv(mem, g, w):
    batch, n_mem, d = mem.shape
    depth = w.shape[0]
    out_spec = lambda r, c: pl.BlockSpec((None, None, CA_HEADS, r, c), lambda l, b: (l, b, 0, 0, 0))
    return pl.pallas_call(
        _mem_kv_kernel,
        grid=(depth, batch),
        in_specs=[
            pl.BlockSpec((None, n_mem, d), lambda l, b: (b, 0, 0)),
            pl.BlockSpec((None, 1, d), lambda l, b: (l, 0, 0)),
            pl.BlockSpec((None, d, 2 * CA_WIDTH), lambda l, b: (l, 0, 0)),
        ],
        out_specs=[out_spec(CA_WIDTH, n_mem), out_spec(n_mem, CA_WIDTH)],
        out_shape=[
            jax.ShapeDtypeStruct((depth, batch, CA_HEADS, CA_WIDTH, n_mem), BF16),
            jax.ShapeDtypeStruct((depth, batch, CA_HEADS, n_mem, CA_WIDTH), BF16),
        ],
        compiler_params=_compiler_params(2),
        name="mem_kv",
    )(mem, g, w)


def _pooled(u_prev_ref, u_ref, u_next_ref, ext_ref, seq):
    tm = u_ref.shape[0]
    tiles_per_seq = seq // tm
    t = pl.program_id(0) % tiles_per_seq
    u = u_ref[...]
    ext_ref[pl.ds(0, POOL_HALO), :] = jnp.where(t > 0, u_prev_ref[...], 0.0)
    ext_ref[pl.ds(POOL_HALO, tm), :] = u
    ext_ref[pl.ds(POOL_HALO + tm, POOL_HALO), :] = jnp.where(t < tiles_per_seq - 1, u_next_ref[...], 0.0)

    def window(lo, hi):
        acc = ext_ref[pl.ds(POOL_HALO + lo, tm), :]
        for d in range(lo + 1, hi):
            acc = acc + ext_ref[pl.ds(POOL_HALO + d, tm), :]
        return acc

    pos = t * tm + lax.broadcasted_iota(jnp.int32, u.shape, 0)
    group = lax.broadcasted_iota(jnp.int32, u.shape, 1) // POOL_GROUP_DIM
    total = jnp.zeros_like(u)
    count = jnp.ones_like(u)
    for gi, w in enumerate(POOL_WINDOWS):
        half = w // 2
        cnt = (jnp.minimum(pos + half, seq) - jnp.maximum(pos - half, 0)).astype(F32)
        total = jnp.where(group == gi, window(-half, half), total)
        count = jnp.where(group == gi, cnt, count)
    return total / count - u


def _merge_kernel(x_ref, u_prev_ref, u_ref, u_next_ref, o_da_ref, qca_ref, kt_ref, v_ref,
                  g_pre_ref, w_gate_ref, b_gate_ref, pool_w_ref, pool_scale_ref,
                  w_pool_ref, w_da_ref, w_ca_ref, w_out_ref, g_post_ref,
                  out_ref, ext_ref, *, seq):
    x = x_ref[...]
    d = x.shape[-1]
    h = _rms_norm(x, g_pre_ref[...]).astype(BF16)
    gates = jax.nn.sigmoid(_dot(h, w_gate_ref[...]) + b_gate_ref[...])

    pooled = _pooled(u_prev_ref, u_ref, u_next_ref, ext_ref, seq).astype(BF16)
    y_pool = (_dot(pooled, pool_w_ref[...]) * pool_scale_ref[...]).astype(BF16)

    qca = qca_ref[...]
    y_ca = jnp.zeros(qca.shape, F32)
    for hd in range(CA_HEADS):
        s = _dot(qca, kt_ref[hd])
        p = jnp.exp2(s - jnp.max(s, axis=-1, keepdims=True))
        p = p * (1.0 / jnp.sum(p, axis=-1, keepdims=True))
        y_ca = y_ca + _dot(p.astype(BF16), v_ref[hd])

    merged = (gates[:, :d] * _dot(y_pool, w_pool_ref[...])
              + gates[:, d:2 * d] * _dot(o_da_ref[...], w_da_ref[...])
              + gates[:, 2 * d:] * _dot(y_ca.astype(BF16), w_ca_ref[...]))
    y = _dot(merged.astype(BF16), w_out_ref[...])
    out_ref[...] = x + _rms_norm(y, g_post_ref[...])


def _merge(x, u, o_da, qca, kt, v, g_pre, w_gate, b_gate, pool_w, pool_scale,
           w_pool, w_da, w_ca, w_out, g_post, layer, seq):
    n, d = x.shape
    n_mem = v.shape[-2]
    tiles_per_seq = seq // TOKEN_TILE
    halo_blocks = TOKEN_TILE // POOL_HALO
    n_halo = n // POOL_HALO
    row = lambda width: pl.BlockSpec((TOKEN_TILE, width), lambda i: (i, 0))
    layer_block = lambda *shape: _resident((None,) + shape, lambda i: (layer,) + (0,) * len(shape))
    mem_block = lambda r, c: pl.BlockSpec((None, None, CA_HEADS, r, c),
                                          lambda i: (layer, i // tiles_per_seq, 0, 0, 0))
    return pl.pallas_call(
        functools.partial(_merge_kernel, seq=seq),
        grid=(n // TOKEN_TILE,),
        in_specs=[
            row(d),
            pl.BlockSpec((POOL_HALO, POOL_WIDTH), lambda i: (jnp.maximum(i * halo_blocks - 1, 0), 0)),
            row(POOL_WIDTH),
            pl.BlockSpec((POOL_HALO, POOL_WIDTH), lambda i: (jnp.minimum((i + 1) * halo_blocks, n_halo - 1), 0)),
            row(DA_WIDTH),
            row(CA_WIDTH),
            mem_block(CA_WIDTH, n_mem),
            mem_block(n_mem, CA_WIDTH),
            layer_block(1, d),
            layer_block(d, 3 * d),
            layer_block(1, 3 * d),
            layer_block(POOL_WIDTH, POOL_WIDTH),
            layer_block(1, POOL_WIDTH),
            layer_block(POOL_WIDTH, d),
            layer_block(DA_WIDTH, d),
            layer_block(CA_WIDTH, d),
            layer_block(d, d),
            layer_block(1, d),
        ],
        out_specs=row(d),
        out_shape=jax.ShapeDtypeStruct((n, d), F32),
        scratch_shapes=[pltpu.VMEM((TOKEN_TILE + 2 * POOL_HALO, POOL_WIDTH), F32)],
        compiler_params=_compiler_params(1),
        name="merge",
    )(x, u, u, u, o_da, qca, kt, v, g_pre, w_gate, b_gate, pool_w, pool_scale,
      w_pool, w_da, w_ca, w_out, g_post)


def _block_diag(w):
    depth, groups, c, _ = w.shape
    eye = jnp.eye(groups, dtype=w.dtype)
    return jnp.einsum("lgcd,gh->lgchd", w, eye).reshape(depth, groups * c, groups * c)


def kernel(x, mem, positions, ffn1_pre_g, ffn1_w_up, ffn1_w_down, ffn1_post_g, mix_pre_g, w_in, pool_w, pool_scale, da_lambda_q1, da_lambda_k1, da_lambda_q2, da_lambda_k2, da_subln_g, mem_norm_g, w_mem_kv, w_gate, b_gate, w_br_pool, w_br_da, w_br_ca, w_out, mix_post_g, ffn2_pre_g, ffn2_w_up, ffn2_w_down, ffn2_post_g):
    batch, seq, d = x.shape
    depth = w_in.shape[0]
    assert seq % TOKEN_TILE == 0 and seq % ATTN_Q_TILE == 0
    assert w_in.shape[-1] == POOL_WIDTH + 3 * DA_WIDTH + CA_WIDTH

    vec = lambda a: a.astype(F32).reshape(depth, 1, a.shape[-1])
    bf = lambda a: a.astype(BF16)

    cos_t, sin_t = _rope_tables(positions)
    kt_all, v_all = _mem_kv(mem, vec(mem_norm_g), bf(w_mem_kv))

    ffn1 = (vec(ffn1_pre_g), bf(ffn1_w_up), bf(ffn1_w_down), vec(ffn1_post_g))
    ffn2 = (vec(ffn2_pre_g), bf(ffn2_w_up), bf(ffn2_w_down), vec(ffn2_post_g))
    mix_g, w_in_b = vec(mix_pre_g), bf(w_in)
    lam_vecs = tuple(vec(a) for a in (da_lambda_q1, da_lambda_k1, da_lambda_q2, da_lambda_k2))
    merge_params = (mix_g, bf(w_gate), vec(b_gate), bf(_block_diag(pool_w)), vec(pool_scale),
                    bf(w_br_pool), bf(w_br_da), bf(w_br_ca), bf(w_out), vec(mix_post_g))
    subln_g = vec(da_subln_g)

    xt = x.reshape(batch * seq, d)
    for layer in range(depth):
        lam_init = 0.8 - 0.6 * math.exp(-0.3 * layer)
        xt = _ffn(xt, *ffn1, layer)
        u, q8, k8, vt, qca = _inproj(xt, mix_g, w_in_b, cos_t, sin_t, layer, batch, seq)
        o_da = _attention(q8, k8, vt, *lam_vecs, subln_g, layer, lam_init)
        xt = _merge(xt, u, o_da, qca, kt_all, v_all, *merge_params, layer, seq)
        xt = _ffn(xt, *ffn2, layer)
    return xt.reshape(batch, seq, d)
```

```python
import functools
import math

import jax
import jax.numpy as jnp
import numpy as np
from jax import lax
from jax.experimental import pallas as pl
from jax.experimental.pallas import tpu as pltpu

F32 = jnp.float32
BF16 = jnp.bfloat16
F8 = jnp.float8_e4m3fn

POOL_WINDOWS = (2, 4, 8, 16)
POOL_WIDTH = 256
POOL_GROUP_DIM = POOL_WIDTH // len(POOL_WINDOWS)
POOL_HALO = max(POOL_WINDOWS) // 2
DA_HEADS = 4
DA_HEAD_DIM = 64
DA_V_DIM = 2 * DA_HEAD_DIM
DA_WIDTH = DA_HEADS * DA_V_DIM
CA_HEADS = 4
CA_HEAD_DIM = 64
CA_WIDTH = CA_HEADS * CA_HEAD_DIM
ROPE_THETA = 500000.0
ROPE_DIM = DA_HEAD_DIM // 4
ROPE_HALF = ROPE_DIM // 2
NORM_EPS = 1e-6
LOG2E = math.log2(math.e)

V7X_LANES = 128
V7X_SUBLANES = 8
V7X_VMEM_LIMIT_BYTES = 60000 * 1024

TOKEN_TILE = 512
MERGE_TILE = 1024
ATTN_Q_TILE = 256
ATTN_K_TILE = 4096
ATTN_ROW_CHUNK = 2048
BF16_SUBLANE_TILE = 2 * V7X_SUBLANES
VT_ROWS = DA_V_DIM + BF16_SUBLANE_TILE
V7X_MXU_DEPTH = 256
QK_STACK = V7X_MXU_DEPTH


def _compiler_params(n_grid_axes):
    return pltpu.CompilerParams(
        dimension_semantics=("arbitrary",) * n_grid_axes,
        vmem_limit_bytes=V7X_VMEM_LIMIT_BYTES,
    )


def _resident(block_shape, index_map):
    return pl.BlockSpec(block_shape, index_map, pipeline_mode=pl.Buffered(1))


def _rms_norm(x, g):
    ms = jnp.mean(x * x, axis=-1, keepdims=True)
    return x * lax.rsqrt(ms + NORM_EPS) * g


def _dot(a, b):
    return jnp.dot(a, b, preferred_element_type=F32)


def _rope_table_kernel(pos_ref, cos_ref, sin_ref):
    pos = pos_ref[...].astype(F32)
    lane = lax.broadcasted_iota(jnp.int32, pos.shape, 1)
    in_comp = lane % DA_HEAD_DIM
    freq_idx = lane % ROPE_HALF
    inv_freq = (np.float32(ROPE_THETA) ** (-np.arange(0, ROPE_DIM, 2, dtype=np.float32) / np.float32(ROPE_DIM)))
    inv = jnp.full(pos.shape, float(inv_freq[0]), F32)
    for j in range(1, ROPE_HALF):
        inv = jnp.where(freq_idx == j, float(inv_freq[j]), inv)
    ang = pos * inv
    rotary = in_comp < ROPE_DIM
    cos_ref[...] = jnp.where(rotary, jnp.cos(ang), 1.0)
    sin_v = jnp.sin(ang)
    sin_ref[...] = jnp.where(rotary, jnp.where(in_comp < ROPE_HALF, -sin_v, sin_v), 0.0)


def _rope_tables(positions):
    n = positions.size
    pos = jnp.broadcast_to(positions.reshape(n, 1), (n, V7X_LANES))
    spec = pl.BlockSpec((TOKEN_TILE, V7X_LANES), lambda i: (i, 0))
    return pl.pallas_call(
        _rope_table_kernel,
        grid=(n // TOKEN_TILE,),
        in_specs=[spec],
        out_specs=[spec, spec],
        out_shape=[jax.ShapeDtypeStruct((n, V7X_LANES), F32)] * 2,
        compiler_params=_compiler_params(1),
        name="rope_tables",
    )(pos)


def _ffn_kernel(x_ref, g_pre_ref, w_up_ref, w_down_ref, g_post_ref, o_ref, *, d_ff):
    x = x_ref[...]
    h = _rms_norm(x, g_pre_ref[...]).astype(BF16)
    ab = _dot(h, w_up_ref[...])
    a = ab[:, :d_ff]
    b = ab[:, d_ff:]
    act = (a * jax.nn.sigmoid(a) * b).astype(BF16)
    y = _dot(act, w_down_ref[...])
    o_ref[...] = x + 0.5 * _rms_norm(y, g_post_ref[...])


def _ffn(x, g_pre, w_up, w_down, g_post, layer):
    n, d = x.shape
    d_ff = w_down.shape[1]
    row = pl.BlockSpec((TOKEN_TILE, d), lambda i: (i, 0))
    gain = _resident((None, 1, d), lambda i: (layer, 0, 0))
    return pl.pallas_call(
        functools.partial(_ffn_kernel, d_ff=d_ff),
        grid=(n // TOKEN_TILE,),
        in_specs=[
            row,
            gain,
            _resident((None, d, 2 * d_ff), lambda i: (layer, 0, 0)),
            _resident((None, d_ff, d), lambda i: (layer, 0, 0)),
            gain,
        ],
        out_specs=row,
        out_shape=jax.ShapeDtypeStruct((n, d), F32),
        compiler_params=_compiler_params(1),
        name="ffn",
    )(x, g_pre, w_up, w_down, g_post)


def _apply_rope(t, cos_t, sin_t):
    width = t.shape[-1]
    lane = lax.broadcasted_iota(jnp.int32, t.shape, 1) % DA_HEAD_DIM
    partner = jnp.where(lane < ROPE_HALF,
                        pltpu.roll(t, width - ROPE_HALF, 1),
                        pltpu.roll(t, ROPE_HALF, 1))
    reps = width // cos_t.shape[-1]
    cos_w = jnp.concatenate([cos_t] * reps, axis=-1)
    sin_w = jnp.concatenate([sin_t] * reps, axis=-1)
    return t * cos_w + partner * sin_w


def _split_fp8(x):
    fp8_max = float(jnp.finfo(F8).max)
    x = jnp.clip(x, -fp8_max, fp8_max)
    hi = x.astype(F8)
    return hi, x - hi.astype(F32)


def _inproj_kernel(x_ref, g_ref, w_ref, cos_ref, sin_ref,
                   u_ref, q8_ref, k8_ref, vt_ref, qca_ref):
    h = _rms_norm(x_ref[...], g_ref[...]).astype(BF16)
    proj = _dot(h, w_ref[...])
    tm = proj.shape[0]
    o_q = POOL_WIDTH
    o_k = o_q + DA_WIDTH
    o_v = o_k + DA_WIDTH
    o_ca = o_v + DA_WIDTH
    cos_t = cos_ref[...]
    sin_t = sin_ref[...]
    u_ref[...] = proj[:, :o_q]
    qk_scale = math.sqrt(DA_HEAD_DIM ** -0.5 * LOG2E)
    q_t = (_apply_rope(proj[:, o_q:o_k], cos_t, sin_t) * qk_scale).T
    k = _apply_rope(proj[:, o_k:o_v], cos_t, sin_t) * qk_scale

    first_half = lax.broadcasted_iota(jnp.int32, (tm, DA_V_DIM), 1) < DA_HEAD_DIM
    zero_rows = jnp.zeros((DA_HEAD_DIM, tm), F8)
    for hd in range(DA_HEADS):
        k_head = k[:, hd * DA_V_DIM:(hd + 1) * DA_V_DIM]
        k_swap = pltpu.roll(k_head, DA_HEAD_DIM, 1)
        for c in range(2):
            both = jnp.where(first_half, k_head, k_swap) if c == 0 else jnp.where(first_half, k_swap, k_head)
            hi, lo = _split_fp8(both)
            col = (2 * hd + c) * QK_STACK
            k8_ref[:, col:col + DA_V_DIM] = hi
            k8_ref[:, col + DA_V_DIM:col + QK_STACK] = jnp.where(first_half, lo, 0.0).astype(F8)
            row = hd * DA_V_DIM + c * DA_HEAD_DIM
            hi, lo = _split_fp8(q_t[row:row + DA_HEAD_DIM])
            base = c * QK_STACK
            q8_ref[hd, base:base + DA_HEAD_DIM] = hi
            q8_ref[hd, base + DA_HEAD_DIM:base + 2 * DA_HEAD_DIM] = lo.astype(F8)
            q8_ref[hd, base + 2 * DA_HEAD_DIM:base + 3 * DA_HEAD_DIM] = hi
            q8_ref[hd, base + 3 * DA_HEAD_DIM:base + QK_STACK] = zero_rows
    vt = proj[:, o_v:o_ca].T.astype(BF16)
    ones_row = (lax.broadcasted_iota(jnp.int32, (BF16_SUBLANE_TILE, vt.shape[1]), 0) == 0).astype(BF16)
    for hd in range(DA_HEADS):
        vt_ref[hd, :DA_V_DIM, :] = vt[hd * DA_V_DIM:(hd + 1) * DA_V_DIM]
        vt_ref[hd, DA_V_DIM:, :] = ones_row
    qca_ref[...] = (proj[:, o_ca:] * (CA_HEAD_DIM ** -0.5 * LOG2E)).astype(BF16)


def _inproj(x, g, w_in, cos_t, sin_t, layer, batch, seq):
    n, d = x.shape
    tiles_per_seq = seq // TOKEN_TILE
    row = lambda width: pl.BlockSpec((TOKEN_TILE, width), lambda i: (i, 0))
    col = lambda rows: pl.BlockSpec((None, DA_HEADS, rows, TOKEN_TILE),
                                    lambda i: (i // tiles_per_seq, 0, 0, i % tiles_per_seq))
    return pl.pallas_call(
        _inproj_kernel,
        grid=(n // TOKEN_TILE,),
        in_specs=[
            row(d),
            _resident((None, 1, d), lambda i: (layer, 0, 0)),
            _resident((None, d, w_in.shape[-1]), lambda i: (layer, 0, 0)),
            row(V7X_LANES),
            row(V7X_LANES),
        ],
        out_specs=[row(POOL_WIDTH), col(2 * QK_STACK), row(DA_HEADS * 2 * QK_STACK), col(VT_ROWS), row(CA_WIDTH)],
        out_shape=[
            jax.ShapeDtypeStruct((n, POOL_WIDTH), F32),
            jax.ShapeDtypeStruct((batch, DA_HEADS, 2 * QK_STACK, seq), F8),
            jax.ShapeDtypeStruct((n, DA_HEADS * 2 * QK_STACK), F8),
            jax.ShapeDtypeStruct((batch, DA_HEADS, VT_ROWS, seq), BF16),
            jax.ShapeDtypeStruct((n, CA_WIDTH), BF16),
        ],
        compiler_params=_compiler_params(1),
        name="inproj",
    )(x, g, w_in, cos_t, sin_t)


def _attn_kernel(q8_ref, k8_ref, vt_ref, lq1_ref, lk1_ref, lq2_ref, lk2_ref, g_ref, o_ref,
                 s_even, s_odd, acc_ref, *, lam_init, seq, tq, tk, chunk):
    n_blk = seq // tk
    n_steps = (seq // tq) * n_blk
    neg_inf_rows = jnp.full((V7X_SUBLANES, tq), -jnp.inf, F32)

    def score_chunk(q_start, k_start, r, s_buf, part_max):
        rows = pl.ds(pl.multiple_of(r, chunk), chunk)
        out = []
        for c in range(2):
            cols = slice(c * QK_STACK, (c + 1) * QK_STACK)
            s = _dot(k8_ref[pl.ds(pl.multiple_of(k_start + r, chunk), chunk), cols],
                     q8_ref[cols, pl.ds(q_start, tq)])
            s_buf[c, rows, :] = s
            s_max = jnp.max(s.reshape(-1, V7X_SUBLANES, tq), axis=0)
            out.append(jnp.maximum(part_max[c], s_max))
        return tuple(out)

    def softmax_chunk(k_start, r, s_buf, m_new):
        rows = pl.ds(pl.multiple_of(r, chunk), chunk)
        vt_rows = vt_ref[:, pl.ds(pl.multiple_of(k_start + r, chunk), chunk)]
        for c in range(2):
            p = jnp.exp2((s_buf[c, rows, :] - m_new[c]).astype(BF16))
            acc_ref[c] += _dot(vt_rows, p)

    def block_max(part_max):
        return tuple(jnp.max(pm, axis=0, keepdims=True) for pm in part_max)

    def step(t, t_next, s_cur, s_next, m, cur_max):
        k_blk = t % n_blk
        k_start = k_blk * tk
        next_q_start = pl.multiple_of((t_next // n_blk) * tq, tq)
        next_k_start = (t_next % n_blk) * tk
        m_new = []
        for c in range(2):
            m_prev = jnp.where(k_blk == 0, -jnp.inf, m[c])
            m_new.append(jnp.maximum(m_prev, cur_max[c]))
            acc_ref[c] = acc_ref[c] * jnp.exp2(m_prev - m_new[c])
        m_new = tuple(m_new)

        def body(ci, part_max):
            r = ci * chunk
            part_max = score_chunk(next_q_start, next_k_start, r, s_next, part_max)
            softmax_chunk(k_start, r, s_cur, m_new)
            return part_max

        part_max = lax.fori_loop(0, tk // chunk, body, (neg_inf_rows, neg_inf_rows))
        return m_new, block_max(part_max)

    lam =(jnp.exp(jnp.sum(lq1_ref[...] * lk1_ref[...], axis=-1, keepdims=True))
           - jnp.exp(jnp.sum(lq2_ref[...] * lk2_ref[...], axis=-1, keepdims=True))
           + lam_init)

    def finalize(q_tile):
        acc0 = acc_ref[0]
        acc1 = acc_ref[1]
        o_t = (acc0[:DA_V_DIM] * (1.0 / acc0[DA_V_DIM:DA_V_DIM + 1])
               - lam * (acc1[:DA_V_DIM] * (1.0 / acc1[DA_V_DIM:DA_V_DIM + 1])))
        o = _rms_norm(o_t.T, g_ref[...]) * (1.0 - lam_init)
        o_ref[pl.ds(pl.multiple_of(q_tile * tq, tq), tq), :] = o.astype(BF16)

    acc_ref[...] = jnp.zeros(acc_ref.shape, F32)
    first_max = lax.fori_loop(
        0, tk // chunk,
        lambda ci, pm: score_chunk(0, 0, ci * chunk, s_even, pm),
        (neg_inf_rows, neg_inf_rows))
    neg_inf = jnp.full((1, tq), -jnp.inf, F32)

    def pair(i, carry):
        m, max_even = carry
        t = 2 * i
        m, max_odd = step(t, t + 1, s_even, s_odd, m, max_even)
        m, max_even = step(t + 1, jnp.where(t + 2 == n_steps, 0, t + 2), s_odd, s_even, m, max_odd)

        @pl.when((t + 1) % n_blk == n_blk - 1)
        def _():
            finalize((t + 1) // n_blk)

        return m, max_even

    lax.fori_loop(0, n_steps // 2, pair, ((neg_inf, neg_inf), block_max(first_max)))


def _attention(q8, k8, vt, lq1, lk1, lq2, lk2, g, layer, lam_init):
    batch, _, _, seq = q8.shape
    k8 = k8.reshape(batch, seq, DA_HEADS * 2 * QK_STACK)
    tq = ATTN_Q_TILE
    tk = min(ATTN_K_TILE, seq // 2)
    chunk = min(ATTN_ROW_CHUNK, tk)
    assert seq % tq == 0 and seq % (2 * tk) == 0 and tk % chunk == 0
    small = lambda width: _resident((None, 1, width), lambda b, h: (layer, 0, 0))
    out = pl.pallas_call(
        functools.partial(_attn_kernel, lam_init=lam_init, seq=seq, tq=tq, tk=tk, chunk=chunk),
        grid=(batch, DA_HEADS),
        in_specs=[
            pl.BlockSpec((None, None, 2 * QK_STACK, seq), lambda b, h: (b, h, 0, 0)),
            pl.BlockSpec((None, seq, 2 * QK_STACK), lambda b, h: (b, 0, h)),
            pl.BlockSpec((None, None, VT_ROWS, seq), lambda b, h: (b, h, 0, 0)),
            small(DA_HEAD_DIM), small(DA_HEAD_DIM), small(DA_HEAD_DIM), small(DA_HEAD_DIM),
            small(DA_V_DIM),
        ],
        out_specs=pl.BlockSpec((None, seq, DA_V_DIM), lambda b, h: (b, 0, h)),
        out_shape=jax.ShapeDtypeStruct((batch, seq, DA_WIDTH), BF16),
        scratch_shapes=[
            pltpu.VMEM((2, tk, tq), F32),
            pltpu.VMEM((2, tk, tq), F32),
            pltpu.VMEM((2, VT_ROWS, tq), F32),
        ],
        compiler_params=_compiler_params(2),
        name="diff_attention",
    )(q8, k8, vt, lq1, lk1, lq2, lk2, g)
    return out.reshape(batch * seq, DA_WIDTH)


def _mem_kv_kernel(mem_ref, g_ref, w_ref, kt_ref, v_ref):
    hm = _rms_norm(mem_ref[...], g_ref[...]).astype(BF16)
    kv = _dot(hm, w_ref[...])
    k_t = kv[:, :CA_WIDTH].T
    v = kv[:, CA_WIDTH:]
    k_head = lax.broadcasted_iota(jnp.int32, k_t.shape, 0) // CA_HEAD_DIM
    v_head = lax.broadcasted_iota(jnp.int32, v.shape, 1) // CA_HEAD_DIM
    for h in range(CA_HEADS):
        kt_ref[h] = jnp.where(k_head == h, k_t, 0.0).astype(BF16)
        v_ref[h] = jnp.where(v_head == h, v, 0.0).astype(BF16)


def _mem_kv(mem, g, w):
    batch, n_mem, d = mem.shape
    depth = w.shape[0]
    out_spec = lambda r, c: pl.BlockSpec((None, None, CA_HEADS, r, c), lambda l, b: (l, b, 0, 0, 0))
    return pl.pallas_call(
        _mem_kv_kernel,
        grid=(depth, batch),
        in_specs=[
            pl.BlockSpec((None, n_mem, d), lambda l, b: (b, 0, 0)),
            pl.BlockSpec((None, 1, d), lambda l, b: (l, 0, 0)),
            pl.BlockSpec((None, d, 2 * CA_WIDTH), lambda l, b: (l, 0, 0)),
        ],
        out_specs=[out_spec(CA_WIDTH, n_mem), out_spec(n_mem, CA_WIDTH)],
        out_shape=[
            jax.ShapeDtypeStruct((depth, batch, CA_HEADS, CA_WIDTH, n_mem), BF16),
            jax.ShapeDtypeStruct((depth, batch, CA_HEADS, n_mem, CA_WIDTH), BF16),
        ],
        compiler_params=_compiler_params(2),
        name="mem_kv",
    )(mem, g, w)


def _pooled(u_prev_ref, u_ref, u_next_ref, ext_ref, seq):
    tm = u_ref.shape[0]
    tiles_per_seq = seq // tm
    t = pl.program_id(0) % tiles_per_seq
    u = u_ref[...]
    ext_ref[pl.ds(0, POOL_HALO), :] = jnp.where(t > 0, u_prev_ref[...], 0.0)
    ext_ref[pl.ds(POOL_HALO, tm), :] = u
    ext_ref[pl.ds(POOL_HALO + tm, POOL_HALO), :] = jnp.where(t < tiles_per_seq - 1, u_next_ref[...], 0.0)

    def window(lo, hi):
        acc = ext_ref[pl.ds(POOL_HALO + lo, tm), :]
        for d in range(lo + 1, hi):
            acc = acc + ext_ref[pl.ds(POOL_HALO + d, tm), :]
        return acc

    pos = t * tm + lax.broadcasted_iota(jnp.int32, u.shape, 0)
    group = lax.broadcasted_iota(jnp.int32, u.shape, 1) // POOL_GROUP_DIM
    total = jnp.zeros_like(u)
    count = jnp.ones_like(u)
    for gi, w in enumerate(POOL_WINDOWS):
        half = w // 2
        cnt = (jnp.minimum(pos + half, seq) - jnp.maximum(pos - half, 0)).astype(F32)
        total = jnp.where(group == gi, window(-half, half), total)
        count = jnp.where(group == gi, cnt, count)
    return total / count - u


def _merge_kernel(x_ref, u_prev_ref, u_ref, u_next_ref, o_da_ref, qca_ref, kt_ref, v_ref,
                  g_pre_ref, w_gate_ref, b_gate_ref, pool_w_ref, pool_scale_ref,
                  w_pool_ref, w_da_ref, w_ca_ref, w_out_ref, g_post_ref,
                  out_ref, ext_ref, *, seq):
    x = x_ref[...]
    d = x.shape[-1]
    h = _rms_norm(x, g_pre_ref[...]).astype(BF16)
    gates = jax.nn.sigmoid(_dot(h, w_gate_ref[...]) + b_gate_ref[...])

    pooled = _pooled(u_prev_ref, u_ref, u_next_ref, ext_ref, seq).astype(BF16)
    y_pool = (_dot(pooled, pool_w_ref[...]) * pool_scale_ref[...]).astype(BF16)

    qca = qca_ref[...]
    y_ca = jnp.zeros(qca.shape, F32)
    for hd in range(CA_HEADS):
        s = _dot(qca, kt_ref[hd])
        p = jnp.exp2(s - jnp.max(s, axis=-1, keepdims=True))
        p = p * (1.0 / jnp.sum(p, axis=-1, keepdims=True))
        y_ca = y_ca + _dot(p.astype(BF16), v_ref[hd])

    merged = (gates[:, :d] * _dot(y_pool, w_pool_ref[...])
              + gates[:, d:2 * d] * _dot(o_da_ref[...], w_da_ref[...])
              + gates[:, 2 * d:] * _dot(y_ca.astype(BF16), w_ca_ref[...]))
    y = _dot(merged.astype(BF16), w_out_ref[...])
    out_ref[...] = x + _rms_norm(y, g_post_ref[...])


def _merge(x, u, o_da, qca, kt, v, g_pre, w_gate, b_gate, pool_w, pool_scale,
           w_pool, w_da, w_ca, w_out, g_post, layer, seq):
    n, d = x.shape
    n_mem = v.shape[-2]
    tm = min(MERGE_TILE, seq)
    tiles_per_seq = seq // tm
    halo_blocks = tm // POOL_HALO
    n_halo = n // POOL_HALO
    row = lambda width: pl.BlockSpec((tm, width), lambda i: (i, 0))
    layer_block = lambda *shape: _resident((None,) + shape, lambda i: (layer,) + (0,) * len(shape))
    mem_block = lambda r, c: pl.BlockSpec((None, None, CA_HEADS, r, c),
                                          lambda i: (layer, i // tiles_per_seq, 0, 0, 0))
    return pl.pallas_call(
        functools.partial(_merge_kernel, seq=seq),
        grid=(n // tm,),
        in_specs=[
            row(d),
            pl.BlockSpec((POOL_HALO, POOL_WIDTH), lambda i: (jnp.maximum(i * halo_blocks - 1, 0), 0)),
            row(POOL_WIDTH),
            pl.BlockSpec((POOL_HALO, POOL_WIDTH), lambda i: (jnp.minimum((i + 1) * halo_blocks, n_halo - 1), 0)),
            row(DA_WIDTH),
            row(CA_WIDTH),
            mem_block(CA_WIDTH, n_mem),
            mem_block(n_mem, CA_WIDTH),
            layer_block(1, d),
            layer_block(d, 3 * d),
            layer_block(1, 3 * d),
            layer_block(POOL_WIDTH, POOL_WIDTH),
            layer_block(1, POOL_WIDTH),
            layer_block(POOL_WIDTH, d),
            layer_block(DA_WIDTH, d),
            layer_block(CA_WIDTH, d),
            layer_block(d, d),
            layer_block(1, d),
        ],
        out_specs=row(d),
        out_shape=jax.ShapeDtypeStruct((n, d), F32),
        scratch_shapes=[pltpu.VMEM((tm + 2 * POOL_HALO, POOL_WIDTH), F32)],
        compiler_params=_compiler_params(1),
        name="merge",
    )(x, u, u, u, o_da, qca, kt, v, g_pre, w_gate, b_gate, pool_w, pool_scale,
      w_pool, w_da, w_ca, w_out, g_post)


def _block_diag(w):
    depth, groups, c, _ = w.shape
    eye = jnp.eye(groups, dtype=w.dtype)
    return jnp.einsum("lgcd,gh->lgchd", w, eye).reshape(depth, groups * c, groups * c)


def kernel(x, mem, positions, ffn1_pre_g, ffn1_w_up, ffn1_w_down, ffn1_post_g, mix_pre_g, w_in, pool_w, pool_scale, da_lambda_q1, da_lambda_k1, da_lambda_q2, da_lambda_k2, da_subln_g, mem_norm_g, w_mem_kv, w_gate, b_gate, w_br_pool, w_br_da, w_br_ca, w_out, mix_post_g, ffn2_pre_g, ffn2_w_up, ffn2_w_down, ffn2_post_g):
    batch, seq, d = x.shape
    depth = w_in.shape[0]
    assert seq % TOKEN_TILE == 0 and seq % min(MERGE_TILE, seq) == 0 and seq % ATTN_Q_TILE == 0
    assert w_in.shape[-1] == POOL_WIDTH + 3 * DA_WIDTH + CA_WIDTH

    vec = lambda a: a.astype(F32).reshape(depth, 1, a.shape[-1])
    bf = lambda a: a.astype(BF16)

    cos_t, sin_t = _rope_tables(positions)
    kt_all, v_all = _mem_kv(mem, vec(mem_norm_g), bf(w_mem_kv))

    ffn1 = (vec(ffn1_pre_g), bf(ffn1_w_up), bf(ffn1_w_down), vec(ffn1_post_g))
    ffn2 = (vec(ffn2_pre_g), bf(ffn2_w_up), bf(ffn2_w_down), vec(ffn2_post_g))
    mix_g, w_in_b = vec(mix_pre_g), bf(w_in)
    lam_vecs = tuple(vec(a) for a in (da_lambda_q1, da_lambda_k1, da_lambda_q2, da_lambda_k2))
    merge_params = (mix_g, bf(w_gate), vec(b_gate), bf(_block_diag(pool_w)), vec(pool_scale),
                    bf(w_br_pool), bf(w_br_da), bf(w_br_ca), bf(w_out), vec(mix_post_g))
    subln_g = vec(da_subln_g)

    xt = x.reshape(batch * seq, d)
    for layer in range(depth):
        lam_init = 0.8 - 0.6 * math.exp(-0.3 * layer)
        xt = _ffn(xt, *ffn1, layer)
        u, q8, k8, vt, qca = _inproj(xt, mix_g, w_in_b, cos_t, sin_t, layer, batch, seq)
        o_da = _attention(q8, k8, vt, *lam_vecs, subln_g, layer, lam_init)
        xt = _merge(xt, u, o_da, qca, kt_all, v_all, *merge_params, layer, seq)
        xt = _ffn(xt, *ffn2, layer)
    return xt.reshape(batch, seq, d)
```

```python
import functools
import math

import jax
import jax.numpy as jnp
import numpy as np
from jax import lax
from jax.experimental import pallas as pl
from jax.experimental.pallas import tpu as pltpu

F32 = jnp.float32
BF16 = jnp.bfloat16
F8 = jnp.float8_e4m3fn

POOL_WINDOWS = (2, 4, 8, 16)
POOL_WIDTH = 256
POOL_GROUP_DIM = POOL_WIDTH // len(POOL_WINDOWS)
POOL_HALO = max(POOL_WINDOWS) // 2
DA_HEADS = 4
DA_HEAD_DIM = 64
DA_V_DIM = 2 * DA_HEAD_DIM
DA_WIDTH = DA_HEADS * DA_V_DIM
CA_HEADS = 4
CA_HEAD_DIM = 64
CA_WIDTH = CA_HEADS * CA_HEAD_DIM
ROPE_THETA = 500000.0
ROPE_DIM = DA_HEAD_DIM // 4
ROPE_HALF = ROPE_DIM // 2
NORM_EPS = 1e-6
LOG2E = math.log2(math.e)

V7X_LANES = 128
V7X_SUBLANES = 8
V7X_VMEM_LIMIT_BYTES = 60000 * 1024

TOKEN_TILE = 512
MERGE_TILE = 1024
ATTN_Q_TILE = 512
ATTN_K_TILE = 2048
ATTN_ROW_CHUNK = 2048
BF16_SUBLANE_TILE = 2 * V7X_SUBLANES
VT_ROWS = DA_V_DIM + BF16_SUBLANE_TILE
V7X_MXU_DEPTH = 256
QK_STACK = V7X_MXU_DEPTH


def _compiler_params(n_grid_axes):
    return pltpu.CompilerParams(
        dimension_semantics=("arbitrary",) * n_grid_axes,
        vmem_limit_bytes=V7X_VMEM_LIMIT_BYTES,
    )


def _resident(block_shape, index_map):
    return pl.BlockSpec(block_shape, index_map, pipeline_mode=pl.Buffered(1))


def _rms_norm(x, g):
    ms = jnp.mean(x * x, axis=-1, keepdims=True)
    return x * lax.rsqrt(ms + NORM_EPS) * g


def _dot(a, b):
    return jnp.dot(a, b, preferred_element_type=F32)


def _rope_table_kernel(pos_ref, cos_ref, sin_ref):
    pos = pos_ref[...].astype(F32)
    lane = lax.broadcasted_iota(jnp.int32, pos.shape, 1)
    in_comp = lane % DA_HEAD_DIM
    freq_idx = lane % ROPE_HALF
    inv_freq = (np.float32(ROPE_THETA) ** (-np.arange(0, ROPE_DIM, 2, dtype=np.float32) / np.float32(ROPE_DIM)))
    inv = jnp.full(pos.shape, float(inv_freq[0]), F32)
    for j in range(1, ROPE_HALF):
        inv = jnp.where(freq_idx == j, float(inv_freq[j]), inv)
    ang = pos * inv
    rotary = in_comp < ROPE_DIM
    cos_ref[...] = jnp.where(rotary, jnp.cos(ang), 1.0)
    sin_v = jnp.sin(ang)
    sin_ref[...] = jnp.where(rotary, jnp.where(in_comp < ROPE_HALF, -sin_v, sin_v), 0.0)


def _rope_tables(positions):
    n = positions.size
    pos = jnp.broadcast_to(positions.reshape(n, 1), (n, V7X_LANES))
    spec = pl.BlockSpec((TOKEN_TILE, V7X_LANES), lambda i: (i, 0))
    return pl.pallas_call(
        _rope_table_kernel,
        grid=(n // TOKEN_TILE,),
        in_specs=[spec],
        out_specs=[spec, spec],
        out_shape=[jax.ShapeDtypeStruct((n, V7X_LANES), F32)] * 2,
        compiler_params=_compiler_params(1),
        name="rope_tables",
    )(pos)


def _ffn_kernel(x_ref, g_pre_ref, w_up_ref, w_down_ref, g_post_ref, o_ref, *, d_ff):
    x = x_ref[...]
    h = _rms_norm(x, g_pre_ref[...]).astype(BF16)
    ab = _dot(h, w_up_ref[...])
    a = ab[:, :d_ff]
    b = ab[:, d_ff:]
    act = (a * jax.nn.sigmoid(a) * b).astype(BF16)
    y = _dot(act, w_down_ref[...])
    o_ref[...] = x + 0.5 * _rms_norm(y, g_post_ref[...])


def _ffn(x, g_pre, w_up, w_down, g_post, layer):
    n, d = x.shape
    d_ff = w_down.shape[1]
    row = pl.BlockSpec((TOKEN_TILE, d), lambda i: (i, 0))
    gain = _resident((None, 1, d), lambda i: (layer, 0, 0))
    return pl.pallas_call(
        functools.partial(_ffn_kernel, d_ff=d_ff),
        grid=(n // TOKEN_TILE,),
        in_specs=[
            row,
            gain,
            _resident((None, d, 2 * d_ff), lambda i: (layer, 0, 0)),
            _resident((None, d_ff, d), lambda i: (layer, 0, 0)),
            gain,
        ],
        out_specs=row,
        out_shape=jax.ShapeDtypeStruct((n, d), F32),
        compiler_params=_compiler_params(1),
        name="ffn",
    )(x, g_pre, w_up, w_down, g_post)


def _apply_rope(t, cos_t, sin_t):
    width = t.shape[-1]
    lane = lax.broadcasted_iota(jnp.int32, t.shape, 1) % DA_HEAD_DIM
    partner = jnp.where(lane < ROPE_HALF,
                        pltpu.roll(t, width - ROPE_HALF, 1),
                        pltpu.roll(t, ROPE_HALF, 1))
    reps = width // cos_t.shape[-1]
    cos_w = jnp.concatenate([cos_t] * reps, axis=-1)
    sin_w = jnp.concatenate([sin_t] * reps, axis=-1)
    return t * cos_w + partner * sin_w


def _split_fp8(x):
    fp8_max = float(jnp.finfo(F8).max)
    x = jnp.clip(x, -fp8_max, fp8_max)
    hi = x.astype(F8)
    return hi, x - hi.astype(F32)


def _inproj_kernel(x_ref, g_ref, w_ref, cos_ref, sin_ref,
                   u_ref, q8_ref, k8_ref, vt_ref, qca_ref):
    h = _rms_norm(x_ref[...], g_ref[...]).astype(BF16)
    proj = _dot(h, w_ref[...])
    tm = proj.shape[0]
    o_q = POOL_WIDTH
    o_k = o_q + DA_WIDTH
    o_v = o_k + DA_WIDTH
    o_ca = o_v + DA_WIDTH
    cos_t = cos_ref[...]
    sin_t = sin_ref[...]
    u_ref[...] = proj[:, :o_q]
    qk_scale = math.sqrt(DA_HEAD_DIM ** -0.5 * LOG2E)
    q_t = (_apply_rope(proj[:, o_q:o_k], cos_t, sin_t) * qk_scale).T
    k = _apply_rope(proj[:, o_k:o_v], cos_t, sin_t) * qk_scale

    first_half = lax.broadcasted_iota(jnp.int32, (tm, DA_V_DIM), 1) < DA_HEAD_DIM
    zero_rows = jnp.zeros((DA_HEAD_DIM, tm), F8)
    for hd in range(DA_HEADS):
        k_head = k[:, hd * DA_V_DIM:(hd + 1) * DA_V_DIM]
        k_swap = pltpu.roll(k_head, DA_HEAD_DIM, 1)
        for c in range(2):
            both = jnp.where(first_half, k_head, k_swap) if c == 0 else jnp.where(first_half, k_swap, k_head)
            hi, lo = _split_fp8(both)
            col = (2 * hd + c) * QK_STACK
            k8_ref[:, col:col + DA_V_DIM] = hi
            k8_ref[:, col + DA_V_DIM:col + QK_STACK] = jnp.where(first_half, lo, 0.0).astype(F8)
            row = hd * DA_V_DIM + c * DA_HEAD_DIM
            hi, lo = _split_fp8(q_t[row:row + DA_HEAD_DIM])
            base = c * QK_STACK
            q8_ref[hd, base:base + DA_HEAD_DIM] = hi
            q8_ref[hd, base + DA_HEAD_DIM:base + 2 * DA_HEAD_DIM] = lo.astype(F8)
            q8_ref[hd, base + 2 * DA_HEAD_DIM:base + 3 * DA_HEAD_DIM] = hi
            q8_ref[hd, base + 3 * DA_HEAD_DIM:base + QK_STACK] = zero_rows
    vt = proj[:, o_v:o_ca].T.astype(BF16)
    ones_row = (lax.broadcasted_iota(jnp.int32, (BF16_SUBLANE_TILE, vt.shape[1]), 0) == 0).astype(BF16)
    for hd in range(DA_HEADS):
        vt_ref[hd, :DA_V_DIM, :] = vt[hd * DA_V_DIM:(hd + 1) * DA_V_DIM]
        vt_ref[hd, DA_V_DIM:, :] = ones_row
    qca_ref[...] = (proj[:, o_ca:] * (CA_HEAD_DIM ** -0.5 * LOG2E)).astype(BF16)


def _inproj(x, g, w_in, cos_t, sin_t, layer, batch, seq):
    n, d = x.shape
    tiles_per_seq = seq // TOKEN_TILE
    row = lambda width: pl.BlockSpec((TOKEN_TILE, width), lambda i: (i, 0))
    col = lambda rows: pl.BlockSpec((None, DA_HEADS, rows, TOKEN_TILE),
                                    lambda i: (i // tiles_per_seq, 0, 0, i % tiles_per_seq))
    return pl.pallas_call(
        _inproj_kernel,
        grid=(n // TOKEN_TILE,),
        in_specs=[
            row(d),
            _resident((None, 1, d), lambda i: (layer, 0, 0)),
            _resident((None, d, w_in.shape[-1]), lambda i: (layer, 0, 0)),
            row(V7X_LANES),
            row(V7X_LANES),
        ],
        out_specs=[row(POOL_WIDTH), col(2 * QK_STACK), row(DA_HEADS * 2 * QK_STACK), col(VT_ROWS), row(CA_WIDTH)],
        out_shape=[
            jax.ShapeDtypeStruct((n, POOL_WIDTH), F32),
            jax.ShapeDtypeStruct((batch, DA_HEADS, 2 * QK_STACK, seq), F8),
            jax.ShapeDtypeStruct((n, DA_HEADS * 2 * QK_STACK), F8),
            jax.ShapeDtypeStruct((batch, DA_HEADS, VT_ROWS, seq), BF16),
            jax.ShapeDtypeStruct((n, CA_WIDTH), BF16),
        ],
        compiler_params=_compiler_params(1),
        name="inproj",
    )(x, g, w_in, cos_t, sin_t)


def _attn_kernel(q8_ref, k8_ref, vt_ref, lq1_ref, lk1_ref, lq2_ref, lk2_ref, g_ref, o_ref,
                 s_even, s_odd, acc_ref, *, lam_init, seq, tq, tk, chunk):
    n_blk = seq // tk
    n_steps = (seq // tq) * n_blk
    neg_inf_rows = jnp.full((V7X_SUBLANES, tq), -jnp.inf, F32)

    def score_chunk(q_start, k_start, r, s_buf, part_max):
        rows = pl.ds(pl.multiple_of(r, chunk), chunk)
        out = []
        for c in range(2):
            cols = slice(c * QK_STACK, (c + 1) * QK_STACK)
            s = _dot(k8_ref[pl.ds(pl.multiple_of(k_start + r, chunk), chunk), cols],
                     q8_ref[cols, pl.ds(q_start, tq)])
            s_buf[c, rows, :] = s
            s_max = jnp.max(s.reshape(-1, V7X_SUBLANES, tq), axis=0)
            out.append(jnp.maximum(part_max[c], s_max))
        return tuple(out)

    def softmax_chunk(k_start, r, s_buf, m_new):
        rows = pl.ds(pl.multiple_of(r, chunk), chunk)
        vt_rows = vt_ref[:, pl.ds(pl.multiple_of(k_start + r, chunk), chunk)]
        for c in range(2):
            p = jnp.exp2((s_buf[c, rows, :] - m_new[c]).astype(BF16))
            acc_ref[c] += _dot(vt_rows, p)

    def block_max(part_max):
        return tuple(jnp.max(pm, axis=0, keepdims=True) for pm in part_max)

    def step(t, t_next, s_cur, s_next, m, cur_max):
        k_blk = t % n_blk
        k_start = k_blk * tk
        next_q_start = pl.multiple_of((t_next // n_blk) * tq, tq)
        next_k_start = (t_next % n_blk) * tk
        m_new = []
        for c in range(2):
            m_prev = jnp.where(k_blk == 0, -jnp.inf, m[c])
            m_new.append(jnp.maximum(m_prev, cur_max[c]))
            acc_ref[c] = acc_ref[c] * jnp.exp2(m_prev - m_new[c])
        m_new = tuple(m_new)

        def body(ci, part_max):
            r = ci * chunk
            part_max = score_chunk(next_q_start, next_k_start, r, s_next, part_max)
            softmax_chunk(k_start, r, s_cur, m_new)
            return part_max

        part_max = lax.fori_loop(0, tk // chunk, body, (neg_inf_rows, neg_inf_rows))
        return m_new, block_max(part_max)

    lam =(jnp.exp(jnp.sum(lq1_ref[...] * lk1_ref[...], axis=-1, keepdims=True))
           - jnp.exp(jnp.sum(lq2_ref[...] * lk2_ref[...], axis=-1, keepdims=True))
           + lam_init)

    def finalize(q_tile):
        acc0 = acc_ref[0]
        acc1 = acc_ref[1]
        o_t = (acc0[:DA_V_DIM] * (1.0 / acc0[DA_V_DIM:DA_V_DIM + 1])
               - lam * (acc1[:DA_V_DIM] * (1.0 / acc1[DA_V_DIM:DA_V_DIM + 1])))
        o = _rms_norm(o_t.T, g_ref[...]) * (1.0 - lam_init)
        o_ref[pl.ds(pl.multiple_of(q_tile * tq, tq), tq), :] = o.astype(BF16)

    acc_ref[...] = jnp.zeros(acc_ref.shape, F32)
    first_max = lax.fori_loop(
        0, tk // chunk,
        lambda ci, pm: score_chunk(0, 0, ci * chunk, s_even, pm),
        (neg_inf_rows, neg_inf_rows))
    neg_inf = jnp.full((1, tq), -jnp.inf, F32)

    def pair(i, carry):
        m, max_even = carry
        t = 2 * i
        m, max_odd = step(t, t + 1, s_even, s_odd, m, max_even)
        m, max_even = step(t + 1, jnp.where(t + 2 == n_steps, 0, t + 2), s_odd, s_even, m, max_odd)

        @pl.when((t + 1) % n_blk == n_blk - 1)
        def _():
            finalize((t + 1) // n_blk)

        return m, max_even

    lax.fori_loop(0, n_steps // 2, pair, ((neg_inf, neg_inf), block_max(first_max)))


def _attention(q8, k8, vt, lq1, lk1, lq2, lk2, g, layer, lam_init):
    batch, _, _, seq = q8.shape
    k8 = k8.reshape(batch, seq, DA_HEADS * 2 * QK_STACK)
    tq = ATTN_Q_TILE
    tk = min(ATTN_K_TILE, seq // 2)
    chunk = min(ATTN_ROW_CHUNK, tk)
    assert seq % tq == 0 and seq % (2 * tk) == 0 and tk % chunk == 0
    small = lambda width: _resident((None, 1, width), lambda b, h: (layer, 0, 0))
    out = pl.pallas_call(
        functools.partial(_attn_kernel, lam_init=lam_init, seq=seq, tq=tq, tk=tk, chunk=chunk),
        grid=(batch, DA_HEADS),
        in_specs=[
            pl.BlockSpec((None, None, 2 * QK_STACK, seq), lambda b, h: (b, h, 0, 0)),
            pl.BlockSpec((None, seq, 2 * QK_STACK), lambda b, h: (b, 0, h)),
            pl.BlockSpec((None, None, VT_ROWS, seq), lambda b, h: (b, h, 0, 0)),
            small(DA_HEAD_DIM), small(DA_HEAD_DIM), small(DA_HEAD_DIM), small(DA_HEAD_DIM),
            small(DA_V_DIM),
        ],
        out_specs=pl.BlockSpec((None, seq, DA_V_DIM), lambda b, h: (b, 0, h)),
        out_shape=jax.ShapeDtypeStruct((batch, seq, DA_WIDTH), BF16),
        scratch_shapes=[
            pltpu.VMEM((2, tk, tq), F32),
            pltpu.VMEM((2, tk, tq), F32),
            pltpu.VMEM((2, VT_ROWS, tq), F32),
        ],
        compiler_params=_compiler_params(2),
        name="diff_attention",
    )(q8, k8, vt, lq1, lk1, lq2, lk2, g)
    return out.reshape(batch * seq, DA_WIDTH)


def _mem_kv_kernel(mem_ref, g_ref, w_ref, kt_ref, v_ref):
    hm = _rms_norm(mem_ref[...], g_ref[...]).astype(BF16)
    kv = _dot(hm, w_ref[...])
    k_t = kv[:, :CA_WIDTH].T
    v = kv[:, CA_WIDTH:]
    k_head = lax.broadcasted_iota(jnp.int32, k_t.shape, 0) // CA_HEAD_DIM
    v_head = lax.broadcasted_iota(jnp.int32, v.shape, 1) // CA_HEAD_DIM
    for h in range(CA_HEADS):
        kt_ref[h] = jnp.where(k_head == h, k_t, 0.0).astype(BF16)
        v_ref[h] = jnp.where(v_head == h, v, 0.0).astype(BF16)


def _mem_kv(mem, g, w):
    batch, n_mem, d = mem.shape
    depth = w.shape[0]
    out_spec = lambda r, c: pl.BlockSpec((None, None, CA_HEADS, r, c), lambda l, b: (l, b, 0, 0, 0))
    return pl.pallas_call(
        _mem_kv_kernel,
        grid=(depth, batch),
        in_specs=[
            pl.BlockSpec((None, n_mem, d), lambda l, b: (b, 0, 0)),
            pl.BlockSpec((None, 1, d), lambda l, b: (l, 0, 0)),
            pl.BlockSpec((None, d, 2 * CA_WIDTH), lambda l, b: (l, 0, 0)),
        ],
        out_specs=[out_spec(CA_WIDTH, n_mem), out_spec(n_mem, CA_WIDTH)],
        out_shape=[
            jax.ShapeDtypeStruct((depth, batch, CA_HEADS, CA_WIDTH, n_mem), BF16),
            jax.ShapeDtypeStruct((depth, batch, CA_HEADS, n_mem, CA_WIDTH), BF16),
        ],
        compiler_params=_compiler_params(2),
        name="mem_kv",
    )(mem, g, w)


def _pooled(u_prev_ref, u_ref, u_next_ref, ext_ref, seq):
    tm = u_ref.shape[0]
    tiles_per_seq = seq // tm
    t = pl.program_id(0) % tiles_per_seq
    u = u_ref[...]
    ext_ref[pl.ds(0, POOL_HALO), :] = jnp.where(t > 0, u_prev_ref[...], 0.0)
    ext_ref[pl.ds(POOL_HALO, tm), :] = u
    ext_ref[pl.ds(POOL_HALO + tm, POOL_HALO), :] = jnp.where(t < tiles_per_seq - 1, u_next_ref[...], 0.0)

    def window(lo, hi):
        acc = ext_ref[pl.ds(POOL_HALO + lo, tm), :]
        for d in range(lo + 1, hi):
            acc = acc + ext_ref[pl.ds(POOL_HALO + d, tm), :]
        return acc

    pos = t * tm + lax.broadcasted_iota(jnp.int32, u.shape, 0)
    group = lax.broadcasted_iota(jnp.int32, u.shape, 1) // POOL_GROUP_DIM
    total = jnp.zeros_like(u)
    count = jnp.ones_like(u)
    for gi, w in enumerate(POOL_WINDOWS):
        half = w // 2
        cnt = (jnp.minimum(pos + half, seq) - jnp.maximum(pos - half, 0)).astype(F32)
        total = jnp.where(group == gi, window(-half, half), total)
        count = jnp.where(group == gi, cnt, count)
    return total / count - u


def _merge_kernel(x_ref, u_prev_ref, u_ref, u_next_ref, o_da_ref, qca_ref, kt_ref, v_ref,
                  g_pre_ref, w_gate_ref, b_gate_ref, pool_w_ref, pool_scale_ref,
                  w_pool_ref, w_da_ref, w_ca_ref, w_out_ref, g_post_ref,
                  out_ref, ext_ref, *, seq):
    x = x_ref[...]
    d = x.shape[-1]
    h = _rms_norm(x, g_pre_ref[...]).astype(BF16)
    gates = jax.nn.sigmoid(_dot(h, w_gate_ref[...]) + b_gate_ref[...])

    pooled = _pooled(u_prev_ref, u_ref, u_next_ref, ext_ref, seq).astype(BF16)
    y_pool = (_dot(pooled, pool_w_ref[...]) * pool_scale_ref[...]).astype(BF16)

    qca = qca_ref[...]
    y_ca = jnp.zeros(qca.shape, F32)
    for hd in range(CA_HEADS):
        s = _dot(qca, kt_ref[hd])
        p = jnp.exp2(s - jnp.max(s, axis=-1, keepdims=True))
        p = p * (1.0 / jnp.sum(p, axis=-1, keepdims=True))
        y_ca = y_ca + _dot(p.astype(BF16), v_ref[hd])

    merged = (gates[:, :d] * _dot(y_pool, w_pool_ref[...])
              + gates[:, d:2 * d] * _dot(o_da_ref[...], w_da_ref[...])
              + gates[:, 2 * d:] * _dot(y_ca.astype(BF16), w_ca_ref[...]))
    y = _dot(merged.astype(BF16), w_out_ref[...])
    out_ref[...] = x + _rms_norm(y, g_post_ref[...])


def _merge(x, u, o_da, qca, kt, v, g_pre, w_gate, b_gate, pool_w, pool_scale,
           w_pool, w_da, w_ca, w_out, g_post, layer, seq):
    n, d = x.shape
    n_mem = v.shape[-2]
    tm = min(MERGE_TILE, seq)
    tiles_per_seq = seq // tm
    halo_blocks = tm // POOL_HALO
    n_halo = n // POOL_HALO
    row = lambda width: pl.BlockSpec((tm, width), lambda i: (i, 0))
    layer_block = lambda *shape: _resident((None,) + shape, lambda i: (layer,) + (0,) * len(shape))
    mem_block = lambda r, c: pl.BlockSpec((None, None, CA_HEADS, r, c),
                                          lambda i: (layer, i // tiles_per_seq, 0, 0, 0))
    return pl.pallas_call(
        functools.partial(_merge_kernel, seq=seq),
        grid=(n // tm,),
        in_specs=[
            row(d),
            pl.BlockSpec((POOL_HALO, POOL_WIDTH), lambda i: (jnp.maximum(i * halo_blocks - 1, 0), 0)),
            row(POOL_WIDTH),
            pl.BlockSpec((POOL_HALO, POOL_WIDTH), lambda i: (jnp.minimum((i + 1) * halo_blocks, n_halo - 1), 0)),
            row(DA_WIDTH),
            row(CA_WIDTH),
            mem_block(CA_WIDTH, n_mem),
            mem_block(n_mem, CA_WIDTH),
            layer_block(1, d),
            layer_block(d, 3 * d),
            layer_block(1, 3 * d),
            layer_block(POOL_WIDTH, POOL_WIDTH),
            layer_block(1, POOL_WIDTH),
            layer_block(POOL_WIDTH, d),
            layer_block(DA_WIDTH, d),
            layer_block(CA_WIDTH, d),
            layer_block(d, d),
            layer_block(1, d),
        ],
        out_specs=row(d),
        out_shape=jax.ShapeDtypeStruct((n, d), F32),
        scratch_shapes=[pltpu.VMEM((tm + 2 * POOL_HALO, POOL_WIDTH), F32)],
        compiler_params=_compiler_params(1),
        name="merge",
    )(x, u, u, u, o_da, qca, kt, v, g_pre, w_gate, b_gate, pool_w, pool_scale,
      w_pool, w_da, w_ca, w_out, g_post)


def _block_diag(w):
    depth, groups, c, _ = w.shape
    eye = jnp.eye(groups, dtype=w.dtype)
    return jnp.einsum("lgcd,gh->lgchd", w, eye).reshape(depth, groups * c, groups * c)


def kernel(x, mem, positions, ffn1_pre_g, ffn1_w_up, ffn1_w_down, ffn1_post_g, mix_pre_g, w_in, pool_w, pool_scale, da_lambda_q1, da_lambda_k1, da_lambda_q2, da_lambda_k2, da_subln_g, mem_norm_g, w_mem_kv, w_gate, b_gate, w_br_pool, w_br_da, w_br_ca, w_out, mix_post_g, ffn2_pre_g, ffn2_w_up, ffn2_w_down, ffn2_post_g):
    batch, seq, d = x.shape
    depth = w_in.shape[0]
    assert seq % TOKEN_TILE == 0 and seq % min(MERGE_TILE, seq) == 0 and seq % ATTN_Q_TILE == 0
    assert w_in.shape[-1] == POOL_WIDTH + 3 * DA_WIDTH + CA_WIDTH

    vec = lambda a: a.astype(F32).reshape(depth, 1, a.shape[-1])
    bf = lambda a: a.astype(BF16)

    cos_t, sin_t = _rope_tables(positions)
    kt_all, v_all = _mem_kv(mem, vec(mem_norm_g), bf(w_mem_kv))

    ffn1 = (vec(ffn1_pre_g), bf(ffn1_w_up), bf(ffn1_w_down), vec(ffn1_post_g))
    ffn2 = (vec(ffn2_pre_g), bf(ffn2_w_up), bf(ffn2_w_down), vec(ffn2_post_g))
    mix_g, w_in_b = vec(mix_pre_g), bf(w_in)
    lam_vecs = tuple(vec(a) for a in (da_lambda_q1, da_lambda_k1, da_lambda_q2, da_lambda_k2))
    merge_params = (mix_g, bf(w_gate), vec(b_gate), bf(_block_diag(pool_w)), vec(pool_scale),
                    bf(w_br_pool), bf(w_br_da), bf(w_br_ca), bf(w_out), vec(mix_post_g))
    subln_g = vec(da_subln_g)

    xt = x.reshape(batch * seq, d)
    for layer in range(depth):
        lam_init = 0.8 - 0.6 * math.exp(-0.3 * layer)
        xt = _ffn(xt, *ffn1, layer)
        u, q8, k8, vt, qca = _inproj(xt, mix_g, w_in_b, cos_t, sin_t, layer, batch, seq)
        o_da = _attention(q8, k8, vt, *lam_vecs, subln_g, layer, lam_init)
        xt = _merge(xt, u, o_da, qca, kt_all, v_all, *merge_params, layer, seq)
        xt = _ffn(xt, *ffn2, layer)
    return xt.reshape(batch, seq, d)
```

```python
import functools
import math

import jax
import jax.numpy as jnp
import numpy as np
from jax import lax
from jax.experimental import pallas as pl
from jax.experimental.pallas import tpu as pltpu

F32 = jnp.float32
BF16 = jnp.bfloat16
F8 = jnp.float8_e4m3fn

POOL_WINDOWS = (2, 4, 8, 16)
POOL_WIDTH = 256
POOL_GROUP_DIM = POOL_WIDTH // len(POOL_WINDOWS)
POOL_HALO = max(POOL_WINDOWS) // 2
DA_HEADS = 4
DA_HEAD_DIM = 64
DA_V_DIM = 2 * DA_HEAD_DIM
DA_WIDTH = DA_HEADS * DA_V_DIM
CA_HEADS = 4
CA_HEAD_DIM = 64
CA_WIDTH = CA_HEADS * CA_HEAD_DIM
ROPE_THETA = 500000.0
ROPE_DIM = DA_HEAD_DIM // 4
ROPE_HALF = ROPE_DIM // 2
NORM_EPS = 1e-6
LOG2E = math.log2(math.e)

V7X_LANES = 128
V7X_SUBLANES = 8
V7X_VMEM_LIMIT_BYTES = 60000 * 1024

TOKEN_TILE = 512
WIDE_TILE = 1024
ATTN_Q_TILE = 512
ATTN_K_TILE = 2048
ATTN_ROW_CHUNK = 1024
BF16_SUBLANE_TILE = 2 * V7X_SUBLANES
VT_ROWS = DA_V_DIM + BF16_SUBLANE_TILE
V7X_MXU_DEPTH = 256
QK_STACK = V7X_MXU_DEPTH


def _compiler_params(n_grid_axes):
    return pltpu.CompilerParams(
        dimension_semantics=("arbitrary",) * n_grid_axes,
        vmem_limit_bytes=V7X_VMEM_LIMIT_BYTES,
    )


def _resident(block_shape, index_map):
    return pl.BlockSpec(block_shape, index_map, pipeline_mode=pl.Buffered(1))


def _rms_norm(x, g):
    ms = jnp.mean(x * x, axis=-1, keepdims=True)
    return x * lax.rsqrt(ms + NORM_EPS) * g


def _dot(a, b):
    return jnp.dot(a, b, preferred_element_type=F32)


def _rope_table_kernel(pos_ref, cos_ref, sin_ref):
    pos = pos_ref[...].astype(F32)
    lane = lax.broadcasted_iota(jnp.int32, pos.shape, 1)
    in_comp = lane % DA_HEAD_DIM
    freq_idx = lane % ROPE_HALF
    inv_freq = (np.float32(ROPE_THETA) ** (-np.arange(0, ROPE_DIM, 2, dtype=np.float32) / np.float32(ROPE_DIM)))
    inv = jnp.full(pos.shape, float(inv_freq[0]), F32)
    for j in range(1, ROPE_HALF):
        inv = jnp.where(freq_idx == j, float(inv_freq[j]), inv)
    ang = pos * inv
    rotary = in_comp < ROPE_DIM
    cos_ref[...] = jnp.where(rotary, jnp.cos(ang), 1.0)
    sin_v = jnp.sin(ang)
    sin_ref[...] = jnp.where(rotary, jnp.where(in_comp < ROPE_HALF, -sin_v, sin_v), 0.0)


def _rope_tables(positions):
    n = positions.size
    pos = jnp.broadcast_to(positions.reshape(n, 1), (n, V7X_LANES))
    spec = pl.BlockSpec((TOKEN_TILE, V7X_LANES), lambda i: (i, 0))
    return pl.pallas_call(
        _rope_table_kernel,
        grid=(n // TOKEN_TILE,),
        in_specs=[spec],
        out_specs=[spec, spec],
        out_shape=[jax.ShapeDtypeStruct((n, V7X_LANES), F32)] * 2,
        compiler_params=_compiler_params(1),
        name="rope_tables",
    )(pos)


def _ffn_kernel(x_ref, g_pre_ref, w_up_ref, w_down_ref, g_post_ref, o_ref, *, d_ff):
    x = x_ref[...]
    h = _rms_norm(x, g_pre_ref[...]).astype(BF16)
    ab = _dot(h, w_up_ref[...])
    a = ab[:, :d_ff]
    b = ab[:, d_ff:]
    act = (a * jax.nn.sigmoid(a) * b).astype(BF16)
    y = _dot(act, w_down_ref[...])
    o_ref[...] = x + 0.5 * _rms_norm(y, g_post_ref[...])


def _ffn(x, g_pre, w_up, w_down, g_post, layer):
    n, d = x.shape
    d_ff = w_down.shape[1]
    row = pl.BlockSpec((TOKEN_TILE, d), lambda i: (i, 0))
    gain = _resident((None, 1, d), lambda i: (layer, 0, 0))
    return pl.pallas_call(
        functools.partial(_ffn_kernel, d_ff=d_ff),
        grid=(n // TOKEN_TILE,),
        in_specs=[
            row,
            gain,
            _resident((None, d, 2 * d_ff), lambda i: (layer, 0, 0)),
            _resident((None, d_ff, d), lambda i: (layer, 0, 0)),
            gain,
        ],
        out_specs=row,
        out_shape=jax.ShapeDtypeStruct((n, d), F32),
        compiler_params=_compiler_params(1),
        name="ffn",
    )(x, g_pre, w_up, w_down, g_post)


def _apply_rope(t, cos_t, sin_t):
    width = t.shape[-1]
    lane = lax.broadcasted_iota(jnp.int32, t.shape, 1) % DA_HEAD_DIM
    partner = jnp.where(lane < ROPE_HALF,
                        pltpu.roll(t, width - ROPE_HALF, 1),
                        pltpu.roll(t, ROPE_HALF, 1))
    reps = width // cos_t.shape[-1]
    cos_w = jnp.concatenate([cos_t] * reps, axis=-1)
    sin_w = jnp.concatenate([sin_t] * reps, axis=-1)
    return t * cos_w + partner * sin_w


def _split_fp8(x):
    fp8_max = float(jnp.finfo(F8).max)
    x = jnp.clip(x, -fp8_max, fp8_max)
    hi = x.astype(F8)
    return hi, x - hi.astype(F32)


def _inproj_kernel(x_ref, g_ref, w_ref, cos_ref, sin_ref,
                   u_ref, q8_ref, k8_ref, vt_ref, qca_ref):
    h = _rms_norm(x_ref[...], g_ref[...]).astype(BF16)
    proj = _dot(h, w_ref[...])
    tm = proj.shape[0]
    o_q = POOL_WIDTH
    o_k = o_q + DA_WIDTH
    o_v = o_k + DA_WIDTH
    o_ca = o_v + DA_WIDTH
    cos_t = cos_ref[...]
    sin_t = sin_ref[...]
    u_ref[...] = proj[:, :o_q]
    qk_scale = math.sqrt(DA_HEAD_DIM ** -0.5 * LOG2E)
    q_t = (_apply_rope(proj[:, o_q:o_k], cos_t, sin_t) * qk_scale).T
    k = _apply_rope(proj[:, o_k:o_v], cos_t, sin_t) * qk_scale

    first_half = lax.broadcasted_iota(jnp.int32, (tm, DA_V_DIM), 1) < DA_HEAD_DIM
    zero_rows = jnp.zeros((DA_HEAD_DIM, tm), F8)
    for hd in range(DA_HEADS):
        k_head = k[:, hd * DA_V_DIM:(hd + 1) * DA_V_DIM]
        k_swap = pltpu.roll(k_head, DA_HEAD_DIM, 1)
        for c in range(2):
            both = jnp.where(first_half, k_head, k_swap) if c == 0 else jnp.where(first_half, k_swap, k_head)
            hi, lo = _split_fp8(both)
            col = (2 * hd + c) * QK_STACK
            k8_ref[:, col:col + DA_V_DIM] = hi
            k8_ref[:, col + DA_V_DIM:col + QK_STACK] = jnp.where(first_half, lo, 0.0).astype(F8)
            row = hd * DA_V_DIM + c * DA_HEAD_DIM
            hi, lo = _split_fp8(q_t[row:row + DA_HEAD_DIM])
            base = c * QK_STACK
            q8_ref[hd, base:base + DA_HEAD_DIM] = hi
            q8_ref[hd, base + DA_HEAD_DIM:base + 2 * DA_HEAD_DIM] = lo.astype(F8)
            q8_ref[hd, base + 2 * DA_HEAD_DIM:base + 3 * DA_HEAD_DIM] = hi
            q8_ref[hd, base + 3 * DA_HEAD_DIM:base + QK_STACK] = zero_rows
    vt = proj[:, o_v:o_ca].T.astype(BF16)
    ones_row = (lax.broadcasted_iota(jnp.int32, (BF16_SUBLANE_TILE, vt.shape[1]), 0) == 0).astype(BF16)
    for hd in range(DA_HEADS):
        vt_ref[hd, :DA_V_DIM, :] = vt[hd * DA_V_DIM:(hd + 1) * DA_V_DIM]
        vt_ref[hd, DA_V_DIM:, :] = ones_row
    qca_ref[...] = (proj[:, o_ca:] * (CA_HEAD_DIM ** -0.5 * LOG2E)).astype(BF16)


def _inproj(x, g, w_in, cos_t, sin_t, layer, batch, seq):
    n, d = x.shape
    tm = min(WIDE_TILE, seq)
    tiles_per_seq = seq // tm
    row = lambda width: pl.BlockSpec((tm, width), lambda i: (i, 0))
    col = lambda rows: pl.BlockSpec((None, DA_HEADS, rows, tm),
                                    lambda i: (i // tiles_per_seq, 0, 0, i % tiles_per_seq))
    return pl.pallas_call(
        _inproj_kernel,
        grid=(n // tm,),
        in_specs=[
            row(d),
            _resident((None, 1, d), lambda i: (layer, 0, 0)),
            _resident((None, d, w_in.shape[-1]), lambda i: (layer, 0, 0)),
            row(V7X_LANES),
            row(V7X_LANES),
        ],
        out_specs=[row(POOL_WIDTH), col(2 * QK_STACK), row(DA_HEADS * 2 * QK_STACK), col(VT_ROWS), row(CA_WIDTH)],
        out_shape=[
            jax.ShapeDtypeStruct((n, POOL_WIDTH), F32),
            jax.ShapeDtypeStruct((batch, DA_HEADS, 2 * QK_STACK, seq), F8),
            jax.ShapeDtypeStruct((n, DA_HEADS * 2 * QK_STACK), F8),
            jax.ShapeDtypeStruct((batch, DA_HEADS, VT_ROWS, seq), BF16),
            jax.ShapeDtypeStruct((n, CA_WIDTH), BF16),
        ],
        compiler_params=_compiler_params(1),
        name="inproj",
    )(x, g, w_in, cos_t, sin_t)


def _attn_kernel(q8_ref, k8_ref, vt_ref, lq1_ref, lk1_ref, lq2_ref, lk2_ref, g_ref, o_ref,
                 s_even, s_odd, acc_ref, *, lam_init, seq, tq, tk, chunk):
    n_blk = seq // tk
    n_steps = (seq // tq) * n_blk
    neg_inf_rows = jnp.full((V7X_SUBLANES, tq), -jnp.inf, F32)

    def score_chunk(q_start, k_start, r, s_buf, part_max):
        rows = pl.ds(pl.multiple_of(r, chunk), chunk)
        out = []
        for c in range(2):
            cols = slice(c * QK_STACK, (c + 1) * QK_STACK)
            s = _dot(k8_ref[pl.ds(pl.multiple_of(k_start + r, chunk), chunk), cols],
                     q8_ref[cols, pl.ds(q_start, tq)])
            s_buf[c, rows, :] = s
            s_max = jnp.max(s.reshape(-1, V7X_SUBLANES, tq), axis=0)
            out.append(jnp.maximum(part_max[c], s_max))
        return tuple(out)

    def softmax_chunk(k_start, r, s_buf, m_new):
        rows = pl.ds(pl.multiple_of(r, chunk), chunk)
        vt_rows = vt_ref[:, pl.ds(pl.multiple_of(k_start + r, chunk), chunk)]
        for c in range(2):
            p = jnp.exp2((s_buf[c, rows, :] - m_new[c]).astype(BF16))
            acc_ref[c] += _dot(vt_rows, p)

    def block_max(part_max):
        return tuple(jnp.max(pm, axis=0, keepdims=True) for pm in part_max)

    def step(t, t_next, s_cur, s_next, m, cur_max):
        k_blk = t % n_blk
        k_start = k_blk * tk
        next_q_start = pl.multiple_of((t_next // n_blk) * tq, tq)
        next_k_start = (t_next % n_blk) * tk
        m_new = []
        for c in range(2):
            m_prev = jnp.where(k_blk == 0, -jnp.inf, m[c])
            m_new.append(jnp.maximum(m_prev, cur_max[c]))
            acc_ref[c] = acc_ref[c] * jnp.exp2(m_prev - m_new[c])
        m_new = tuple(m_new)

        def body(ci, part_max):
            r = ci * chunk
            part_max = score_chunk(next_q_start, next_k_start, r, s_next, part_max)
            softmax_chunk(k_start, r, s_cur, m_new)
            return part_max

        part_max = lax.fori_loop(0, tk // chunk, body, (neg_inf_rows, neg_inf_rows))
        return m_new, block_max(part_max)

    lam = (jnp.exp(jnp.sum(lq1_ref[...] * lk1_ref[...], axis=-1, keepdims=True))
           - jnp.exp(jnp.sum(lq2_ref[...] * lk2_ref[...], axis=-1, keepdims=True))
           + lam_init)

    def finalize(q_tile):
        acc0 = acc_ref[0]
        acc1 = acc_ref[1]
        o_t = (acc0[:DA_V_DIM] * (1.0 / acc0[DA_V_DIM:DA_V_DIM + 1])
               - lam * (acc1[:DA_V_DIM] * (1.0 / acc1[DA_V_DIM:DA_V_DIM + 1])))
        o = _rms_norm(o_t.T, g_ref[...]) * (1.0 - lam_init)
        o_ref[pl.ds(pl.multiple_of(q_tile * tq, tq), tq), :] = o.astype(BF16)

    acc_ref[...] = jnp.zeros(acc_ref.shape, F32)
    first_max = lax.fori_loop(
        0, tk // chunk,
        lambda ci, pm: score_chunk(0, 0, ci * chunk, s_even, pm),
        (neg_inf_rows, neg_inf_rows))
    neg_inf = jnp.full((1, tq), -jnp.inf, F32)

    def pair(i, carry):
        m, max_even = carry
        t = 2 * i
        m, max_odd = step(t, t + 1, s_even, s_odd, m, max_even)
        m, max_even = step(t + 1, jnp.where(t + 2 == n_steps, 0, t + 2), s_odd, s_even, m, max_odd)

        @pl.when((t + 1) % n_blk == n_blk - 1)
        def _():
            finalize((t + 1) // n_blk)

        return m, max_even

    lax.fori_loop(0, n_steps // 2, pair, ((neg_inf, neg_inf), block_max(first_max)))


def _attention(q8, k8, vt, lq1, lk1, lq2, lk2, g, layer, lam_init):
    batch, _, _, seq = q8.shape
    k8 = k8.reshape(batch, seq, DA_HEADS * 2 * QK_STACK)
    tq = ATTN_Q_TILE
    tk = min(ATTN_K_TILE, seq // 2)
    chunk = min(ATTN_ROW_CHUNK, tk)
    assert seq % tq == 0 and seq % (2 * tk) == 0 and tk % chunk == 0
    small = lambda width: _resident((None, 1, width), lambda b, h: (layer, 0, 0))
    out = pl.pallas_call(
        functools.partial(_attn_kernel, lam_init=lam_init, seq=seq, tq=tq, tk=tk, chunk=chunk),
        grid=(batch, DA_HEADS),
        in_specs=[
            pl.BlockSpec((None, None, 2 * QK_STACK, seq), lambda b, h: (b, h, 0, 0)),
            pl.BlockSpec((None, seq, 2 * QK_STACK), lambda b, h: (b, 0, h)),
            pl.BlockSpec((None, None, VT_ROWS, seq), lambda b, h: (b, h, 0, 0)),
            small(DA_HEAD_DIM), small(DA_HEAD_DIM), small(DA_HEAD_DIM), small(DA_HEAD_DIM),
            small(DA_V_DIM),
        ],
        out_specs=pl.BlockSpec((None, seq, DA_V_DIM), lambda b, h: (b, 0, h)),
        out_shape=jax.ShapeDtypeStruct((batch, seq, DA_WIDTH), BF16),
        scratch_shapes=[
            pltpu.VMEM((2, tk, tq), F32),
            pltpu.VMEM((2, tk, tq), F32),
            pltpu.VMEM((2, VT_ROWS, tq), F32),
        ],
        compiler_params=_compiler_params(2),
        name="diff_attention",
    )(q8, k8, vt, lq1, lk1, lq2, lk2, g)
    return out.reshape(batch * seq, DA_WIDTH)


def _mem_kv_kernel(mem_ref, g_ref, w_ref, kt_ref, v_ref):
    hm = _rms_norm(mem_ref[...], g_ref[...]).astype(BF16)
    kv = _dot(hm, w_ref[...])
    k_t = kv[:, :CA_WIDTH].T
    v = kv[:, CA_WIDTH:]
    k_head = lax.broadcasted_iota(jnp.int32, k_t.shape, 0) // CA_HEAD_DIM
    v_head = lax.broadcasted_iota(jnp.int32, v.shape, 1) // CA_HEAD_DIM
    for h in range(CA_HEADS):
        kt_ref[h] = jnp.where(k_head == h, k_t, 0.0).astype(BF16)
        v_ref[h] = jnp.where(v_head == h, v, 0.0).astype(BF16)


def _mem_kv(mem, g, w):
    batch, n_mem, d = mem.shape
    depth = w.shape[0]
    out_spec = lambda r, c: pl.BlockSpec((None, None, CA_HEADS, r, c), lambda l, b: (l, b, 0, 0, 0))
    return pl.pallas_call(
        _mem_kv_kernel,
        grid=(depth, batch),
        in_specs=[
            pl.BlockSpec((None, n_mem, d), lambda l, b: (b, 0, 0)),
            pl.BlockSpec((None, 1, d), lambda l, b: (l, 0, 0)),
            pl.BlockSpec((None, d, 2 * CA_WIDTH), lambda l, b: (l, 0, 0)),
        ],
        out_specs=[out_spec(CA_WIDTH, n_mem), out_spec(n_mem, CA_WIDTH)],
        out_shape=[
            jax.ShapeDtypeStruct((depth, batch, CA_HEADS, CA_WIDTH, n_mem), BF16),
            jax.ShapeDtypeStruct((depth, batch, CA_HEADS, n_mem, CA_WIDTH), BF16),
        ],
        compiler_params=_compiler_params(2),
        name="mem_kv",
    )(mem, g, w)


def _pooled(u_prev_ref, u_ref, u_next_ref, ext_ref, seq):
    tm = u_ref.shape[0]
    tiles_per_seq = seq // tm
    t = pl.program_id(0) % tiles_per_seq
    u = u_ref[...]
    ext_ref[pl.ds(0, POOL_HALO), :] = jnp.where(t > 0, u_prev_ref[...], 0.0)
    ext_ref[pl.ds(POOL_HALO, tm), :] = u
    ext_ref[pl.ds(POOL_HALO + tm, POOL_HALO), :] = jnp.where(t < tiles_per_seq - 1, u_next_ref[...], 0.0)

    def window(lo, hi):
        acc = ext_ref[pl.ds(POOL_HALO + lo, tm), :]
        for d in range(lo + 1, hi):
            acc = acc + ext_ref[pl.ds(POOL_HALO + d, tm), :]
        return acc

    pos = t * tm + lax.broadcasted_iota(jnp.int32, u.shape, 0)
    group = lax.broadcasted_iota(jnp.int32, u.shape, 1) // POOL_GROUP_DIM
    total = jnp.zeros_like(u)
    count = jnp.ones_like(u)
    for gi, w in enumerate(POOL_WINDOWS):
        half = w // 2
        cnt = (jnp.minimum(pos + half, seq) - jnp.maximum(pos - half, 0)).astype(F32)
        total = jnp.where(group == gi, window(-half, half), total)
        count = jnp.where(group == gi, cnt, count)
    return total / count - u


def _merge_kernel(x_ref, u_prev_ref, u_ref, u_next_ref, o_da_ref, qca_ref, kt_ref, v_ref,
                  g_pre_ref, w_gate_ref, b_gate_ref, pool_w_ref, pool_scale_ref,
                  w_pool_ref, w_da_ref, w_ca_ref, w_out_ref, g_post_ref,
                  out_ref, ext_ref, *, seq):
    x = x_ref[...]
    d = x.shape[-1]
    h = _rms_norm(x, g_pre_ref[...]).astype(BF16)
    gates = jax.nn.sigmoid(_dot(h, w_gate_ref[...]) + b_gate_ref[...])

    pooled = _pooled(u_prev_ref, u_ref, u_next_ref, ext_ref, seq).astype(BF16)
    y_pool = (_dot(pooled, pool_w_ref[...]) * pool_scale_ref[...]).astype(BF16)

    qca = qca_ref[...]
    y_ca = jnp.zeros(qca.shape, F32)
    for hd in range(CA_HEADS):
        s = _dot(qca, kt_ref[hd])
        p = jnp.exp2(s - jnp.max(s, axis=-1, keepdims=True))
        p = p * (1.0 / jnp.sum(p, axis=-1, keepdims=True))
        y_ca = y_ca + _dot(p.astype(BF16), v_ref[hd])

    merged = (gates[:, :d] * _dot(y_pool, w_pool_ref[...])
              + gates[:, d:2 * d] * _dot(o_da_ref[...], w_da_ref[...])
              + gates[:, 2 * d:] * _dot(y_ca.astype(BF16), w_ca_ref[...]))
    y = _dot(merged.astype(BF16), w_out_ref[...])
    out_ref[...] = x + _rms_norm(y, g_post_ref[...])


def _merge(x, u, o_da, qca, kt, v, g_pre, w_gate, b_gate, pool_w, pool_scale,
           w_pool, w_da, w_ca, w_out, g_post, layer, seq):
    n, d = x.shape
    n_mem = v.shape[-2]
    tm = min(WIDE_TILE, seq)
    tiles_per_seq = seq // tm
    halo_blocks = tm // POOL_HALO
    n_halo = n // POOL_HALO
    row = lambda width: pl.BlockSpec((tm, width), lambda i: (i, 0))
    layer_block = lambda *shape: _resident((None,) + shape, lambda i: (layer,) + (0,) * len(shape))
    mem_block = lambda r, c: pl.BlockSpec((None, None, CA_HEADS, r, c),
                                          lambda i: (layer, i // tiles_per_seq, 0, 0, 0))
    return pl.pallas_call(
        functools.partial(_merge_kernel, seq=seq),
        grid=(n // tm,),
        in_specs=[
            row(d),
            pl.BlockSpec((POOL_HALO, POOL_WIDTH), lambda i: (jnp.maximum(i * halo_blocks - 1, 0), 0)),
            row(POOL_WIDTH),
            pl.BlockSpec((POOL_HALO, POOL_WIDTH), lambda i: (jnp.minimum((i + 1) * halo_blocks, n_halo - 1), 0)),
            row(DA_WIDTH),
            row(CA_WIDTH),
            mem_block(CA_WIDTH, n_mem),
            mem_block(n_mem, CA_WIDTH),
            layer_block(1, d),
            layer_block(d, 3 * d),
            layer_block(1, 3 * d),
            layer_block(POOL_WIDTH, POOL_WIDTH),
            layer_block(1, POOL_WIDTH),
            layer_block(POOL_WIDTH, d),
            layer_block(DA_WIDTH, d),
            layer_block(CA_WIDTH, d),
            layer_block(d, d),
            layer_block(1, d),
        ],
        out_specs=row(d),
        out_shape=jax.ShapeDtypeStruct((n, d), F32),
        scratch_shapes=[pltpu.VMEM((tm + 2 * POOL_HALO, POOL_WIDTH), F32)],
        compiler_params=_compiler_params(1),
        name="merge",
    )(x, u, u, u, o_da, qca, kt, v, g_pre, w_gate, b_gate, pool_w, pool_scale,
      w_pool, w_da, w_ca, w_out, g_post)


def _block_diag(w):
    depth, groups, c, _ = w.shape
    eye = jnp.eye(groups, dtype=w.dtype)
    return jnp.einsum("lgcd,gh->lgchd", w, eye).reshape(depth, groups * c, groups * c)


def kernel(x, mem, positions, ffn1_pre_g, ffn1_w_up, ffn1_w_down, ffn1_post_g, mix_pre_g, w_in, pool_w, pool_scale, da_lambda_q1, da_lambda_k1, da_lambda_q2, da_lambda_k2, da_subln_g, mem_norm_g, w_mem_kv, w_gate, b_gate, w_br_pool, w_br_da, w_br_ca, w_out, mix_post_g, ffn2_pre_g, ffn2_w_up, ffn2_w_down, ffn2_post_g):
    batch, seq, d = x.shape
    depth = w_in.shape[0]
    assert seq % TOKEN_TILE == 0 and seq % min(WIDE_TILE, seq) == 0 and seq % ATTN_Q_TILE == 0
    assert w_in.shape[-1] == POOL_WIDTH + 3 * DA_WIDTH + CA_WIDTH

    vec = lambda a: a.astype(F32).reshape(depth, 1, a.shape[-1])
    bf = lambda a: a.astype(BF16)

    cos_t, sin_t = _rope_tables(positions)
    kt_all, v_all = _mem_kv(mem, vec(mem_norm_g), bf(w_mem_kv))

    ffn1 = (vec(ffn1_pre_g), bf(ffn1_w_up), bf(ffn1_w_down), vec(ffn1_post_g))
    ffn2 = (vec(ffn2_pre_g), bf(ffn2_w_up), bf(ffn2_w_down), vec(ffn2_post_g))
    mix_g, w_in_b = vec(mix_pre_g), bf(w_in)
    lam_vecs = tuple(vec(a) for a in (da_lambda_q1, da_lambda_k1, da_lambda_q2, da_lambda_k2))
    merge_params = (mix_g, bf(w_gate), vec(b_gate), bf(_block_diag(pool_w)), vec(pool_scale),
                    bf(w_br_pool), bf(w_br_da), bf(w_br_ca), bf(w_out), vec(mix_post_g))
    subln_g = vec(da_subln_g)

    xt = x.reshape(batch * seq, d)
    for layer in range(depth):
        lam_init = 0.8 - 0.6 * math.exp(-0.3 * layer)
        xt = _ffn(xt, *ffn1, layer)
        u, q8, k8, vt, qca = _inproj(xt, mix_g, w_in_b, cos_t, sin_t, layer, batch, seq)
        o_da = _attention(q8, k8, vt, *lam_vecs, subln_g, layer, lam_init)
        xt = _merge(xt, u, o_da, qca, kt_all, v_all, *merge_params, layer, seq)
        xt = _ffn(xt, *ffn2, layer)
    return xt.reshape(batch, seq, d)
```

```python
import functools
import math

import jax
import jax.numpy as jnp
import numpy as np
from jax import lax
from jax.experimental import pallas as pl
from jax.experimental.pallas import tpu as pltpu

F32 = jnp.float32
BF16 = jnp.bfloat16
F8 = jnp.float8_e4m3fn

POOL_WINDOWS = (2, 4, 8, 16)
POOL_WIDTH = 256
POOL_GROUP_DIM = POOL_WIDTH // len(POOL_WINDOWS)
POOL_HALO = max(POOL_WINDOWS) // 2
DA_HEADS = 4
DA_HEAD_DIM = 64
DA_V_DIM = 2 * DA_HEAD_DIM
DA_WIDTH = DA_HEADS * DA_V_DIM
CA_HEADS = 4
CA_HEAD_DIM = 64
CA_WIDTH = CA_HEADS * CA_HEAD_DIM
ROPE_THETA = 500000.0
ROPE_DIM = DA_HEAD_DIM // 4
ROPE_HALF = ROPE_DIM // 2
NORM_EPS = 1e-6
LOG2E = math.log2(math.e)

V7X_LANES = 128
V7X_SUBLANES = 8
V7X_VMEM_LIMIT_BYTES = 60000 * 1024

TOKEN_TILE = 512
WIDE_TILE = 1024
ATTN_Q_TILE = 512
ATTN_K_TILE = 2048
ATTN_ROW_CHUNK = 1024
BF16_SUBLANE_TILE = 2 * V7X_SUBLANES
VT_ROWS = DA_V_DIM + BF16_SUBLANE_TILE
V7X_MXU_DEPTH = 256
QK_STACK = V7X_MXU_DEPTH


def _compiler_params(n_grid_axes):
    return pltpu.CompilerParams(
        dimension_semantics=("arbitrary",) * n_grid_axes,
        vmem_limit_bytes=V7X_VMEM_LIMIT_BYTES,
    )


def _resident(block_shape, index_map):
    return pl.BlockSpec(block_shape, index_map, pipeline_mode=pl.Buffered(1))


def _rms_norm(x, g):
    ms = jnp.mean(x * x, axis=-1, keepdims=True)
    return x * lax.rsqrt(ms + NORM_EPS) * g


def _dot(a, b):
    return jnp.dot(a, b, preferred_element_type=F32)


def _dot_t(a_t, b):
    return lax.dot_general(a_t, b, (((0,), (0,)), ((), ())), preferred_element_type=F32)


def _rope_table_kernel(pos_ref, cos_ref, sin_ref):
    pos = pos_ref[...].astype(F32)
    lane = lax.broadcasted_iota(jnp.int32, pos.shape, 1)
    in_comp = lane % DA_HEAD_DIM
    freq_idx = lane % ROPE_HALF
    inv_freq = (np.float32(ROPE_THETA) ** (-np.arange(0, ROPE_DIM, 2, dtype=np.float32) / np.float32(ROPE_DIM)))
    inv = jnp.full(pos.shape, float(inv_freq[0]), F32)
    for j in range(1, ROPE_HALF):
        inv = jnp.where(freq_idx == j, float(inv_freq[j]), inv)
    ang = pos * inv
    rotary = in_comp < ROPE_DIM
    cos_ref[...] = jnp.where(rotary, jnp.cos(ang), 1.0)
    sin_v = jnp.sin(ang)
    sin_ref[...] = jnp.where(rotary, jnp.where(in_comp < ROPE_HALF, -sin_v, sin_v), 0.0)


def _rope_tables(positions):
    n = positions.size
    pos = jnp.broadcast_to(positions.reshape(n, 1), (n, V7X_LANES))
    spec = pl.BlockSpec((TOKEN_TILE, V7X_LANES), lambda i: (i, 0))
    return pl.pallas_call(
        _rope_table_kernel,
        grid=(n // TOKEN_TILE,),
        in_specs=[spec],
        out_specs=[spec, spec],
        out_shape=[jax.ShapeDtypeStruct((n, V7X_LANES), F32)] * 2,
        compiler_params=_compiler_params(1),
        name="rope_tables",
    )(pos)


def _ffn_kernel(x_ref, g_pre_ref, w_up_ref, w_down_ref, g_post_ref, o_ref, *, d_ff):
    x = x_ref[...]
    h = _rms_norm(x, g_pre_ref[...]).astype(BF16)
    ab = _dot(h, w_up_ref[...])
    a = ab[:, :d_ff]
    b = ab[:, d_ff:]
    act = (a * jax.nn.sigmoid(a) * b).astype(BF16)
    y = _dot(act, w_down_ref[...])
    o_ref[...] = x + 0.5 * _rms_norm(y, g_post_ref[...])


def _ffn(x, g_pre, w_up, w_down, g_post, layer):
    n, d = x.shape
    d_ff = w_down.shape[1]
    row = pl.BlockSpec((TOKEN_TILE, d), lambda i: (i, 0))
    gain = _resident((None, 1, d), lambda i: (layer, 0, 0))
    return pl.pallas_call(
        functools.partial(_ffn_kernel, d_ff=d_ff),
        grid=(n // TOKEN_TILE,),
        in_specs=[
            row,
            gain,
            _resident((None, d, 2 * d_ff), lambda i: (layer, 0, 0)),
            _resident((None, d_ff, d), lambda i: (layer, 0, 0)),
            gain,
        ],
        out_specs=row,
        out_shape=jax.ShapeDtypeStruct((n, d), F32),
        compiler_params=_compiler_params(1),
        name="ffn",
    )(x, g_pre, w_up, w_down, g_post)


def _apply_rope(t, cos_t, sin_t):
    width = t.shape[-1]
    lane = lax.broadcasted_iota(jnp.int32, t.shape, 1) % DA_HEAD_DIM
    partner = jnp.where(lane < ROPE_HALF,
                        pltpu.roll(t, width - ROPE_HALF, 1),
                        pltpu.roll(t, ROPE_HALF, 1))
    reps = width // cos_t.shape[-1]
    cos_w = jnp.concatenate([cos_t] * reps, axis=-1)
    sin_w = jnp.concatenate([sin_t] * reps, axis=-1)
    return t * cos_w + partner * sin_w


def _split_fp8(x):
    fp8_max = float(jnp.finfo(F8).max)
    x = jnp.clip(x, -fp8_max, fp8_max)
    hi = x.astype(F8)
    return hi, x - hi.astype(F32)


def _inproj_kernel(x_ref, g_ref, w_ref, cos_ref, sin_ref,
                   u_ref, q8_ref, k8_ref, vt_ref, qca_ref):
    h = _rms_norm(x_ref[...], g_ref[...]).astype(BF16)
    proj = _dot(h, w_ref[...])
    tm = proj.shape[0]
    o_q = POOL_WIDTH
    o_k = o_q + DA_WIDTH
    o_v = o_k + DA_WIDTH
    o_ca = o_v + DA_WIDTH
    cos_t = cos_ref[...]
    sin_t = sin_ref[...]
    u_ref[...] = proj[:, :o_q]
    qk_scale = math.sqrt(DA_HEAD_DIM ** -0.5 * LOG2E)
    q_t = (_apply_rope(proj[:, o_q:o_k], cos_t, sin_t) * qk_scale).T
    k = _apply_rope(proj[:, o_k:o_v], cos_t, sin_t) * qk_scale

    first_half = lax.broadcasted_iota(jnp.int32, (tm, DA_V_DIM), 1) < DA_HEAD_DIM
    zero_rows = jnp.zeros((DA_HEAD_DIM, tm), F8)
    for hd in range(DA_HEADS):
        k_head = k[:, hd * DA_V_DIM:(hd + 1) * DA_V_DIM]
        k_swap = pltpu.roll(k_head, DA_HEAD_DIM, 1)
        for c in range(2):
            both = jnp.where(first_half, k_head, k_swap) if c == 0 else jnp.where(first_half, k_swap, k_head)
            hi, lo = _split_fp8(both)
            col = (2 * hd + c) * QK_STACK
            k8_ref[:, col:col + DA_V_DIM] = hi
            k8_ref[:, col + DA_V_DIM:col + QK_STACK] = jnp.where(first_half, lo, 0.0).astype(F8)
            row = hd * DA_V_DIM + c * DA_HEAD_DIM
            hi, lo = _split_fp8(q_t[row:row + DA_HEAD_DIM])
            base = c * QK_STACK
            q8_ref[hd, base:base + DA_HEAD_DIM] = hi
            q8_ref[hd, base + DA_HEAD_DIM:base + 2 * DA_HEAD_DIM] = lo.astype(F8)
            q8_ref[hd, base + 2 * DA_HEAD_DIM:base + 3 * DA_HEAD_DIM] = hi
            q8_ref[hd, base + 3 * DA_HEAD_DIM:base + QK_STACK] = zero_rows
    vt = proj[:, o_v:o_ca].T.astype(BF16)
    ones_row = (lax.broadcasted_iota(jnp.int32, (BF16_SUBLANE_TILE, vt.shape[1]), 0) == 0).astype(BF16)
    for hd in range(DA_HEADS):
        vt_ref[hd, :DA_V_DIM, :] = vt[hd * DA_V_DIM:(hd + 1) * DA_V_DIM]
        vt_ref[hd, DA_V_DIM:, :] = ones_row
    qca_ref[...] = (proj[:, o_ca:] * (CA_HEAD_DIM ** -0.5 * LOG2E)).astype(BF16)


def _inproj(x, g, w_in, cos_t, sin_t, layer, batch, seq):
    n, d = x.shape
    tm = min(WIDE_TILE, seq)
    tiles_per_seq = seq // tm
    row = lambda width: pl.BlockSpec((tm, width), lambda i: (i, 0))
    col = lambda rows: pl.BlockSpec((None, DA_HEADS, rows, tm),
                                    lambda i: (i // tiles_per_seq, 0, 0, i % tiles_per_seq))
    return pl.pallas_call(
        _inproj_kernel,
        grid=(n // tm,),
        in_specs=[
            row(d),
            _resident((None, 1, d), lambda i: (layer, 0, 0)),
            _resident((None, d, w_in.shape[-1]), lambda i: (layer, 0, 0)),
            row(V7X_LANES),
            row(V7X_LANES),
        ],
        out_specs=[row(POOL_WIDTH), col(2 * QK_STACK), row(DA_HEADS * 2 * QK_STACK), col(VT_ROWS), row(CA_WIDTH)],
        out_shape=[
            jax.ShapeDtypeStruct((n, POOL_WIDTH), F32),
            jax.ShapeDtypeStruct((batch, DA_HEADS, 2 * QK_STACK, seq), F8),
            jax.ShapeDtypeStruct((n, DA_HEADS * 2 * QK_STACK), F8),
            jax.ShapeDtypeStruct((batch, DA_HEADS, VT_ROWS, seq), BF16),
            jax.ShapeDtypeStruct((n, CA_WIDTH), BF16),
        ],
        compiler_params=_compiler_params(1),
        name="inproj",
    )(x, g, w_in, cos_t, sin_t)


def _attn_kernel(q8_ref, k8_ref, vt_ref, lq1_ref, lk1_ref, lq2_ref, lk2_ref, g_ref, o_ref,
                 s_even, s_odd, acc_ref, *, lam_init, seq, tq, tk, chunk):
    n_blk = seq // tk
    n_steps = (seq // tq) * n_blk
    neg_inf_rows = jnp.full((V7X_SUBLANES, tq), -jnp.inf, F32)

    def score_chunk(q_start, k_start, r, s_buf, part_max):
        rows = pl.ds(pl.multiple_of(r, chunk), chunk)
        out = []
        for c in range(2):
            cols = slice(c * QK_STACK, (c + 1) * QK_STACK)
            s = _dot(k8_ref[pl.ds(pl.multiple_of(k_start + r, chunk), chunk), cols],
                     q8_ref[cols, pl.ds(q_start, tq)])
            s_buf[c, rows, :] = s
            s_max = jnp.max(s.reshape(-1, V7X_SUBLANES, tq), axis=0)
            out.append(jnp.maximum(part_max[c], s_max))
        return tuple(out)

    def softmax_chunk(k_start, r, s_buf, m_new):
        rows = pl.ds(pl.multiple_of(r, chunk), chunk)
        vt_rows = vt_ref[:, pl.ds(pl.multiple_of(k_start + r, chunk), chunk)]
        for c in range(2):
            p = jnp.exp2((s_buf[c, rows, :] - m_new[c]).astype(BF16))
            acc_ref[c] += _dot(vt_rows, p)

    def block_max(part_max):
        return tuple(jnp.max(pm, axis=0, keepdims=True) for pm in part_max)

    def step(t, t_next, s_cur, s_next, m, cur_max):
        k_blk = t % n_blk
        k_start = k_blk * tk
        next_q_start = pl.multiple_of((t_next // n_blk) * tq, tq)
        next_k_start = (t_next % n_blk) * tk
        m_new = []
        for c in range(2):
            m_prev = jnp.where(k_blk == 0, -jnp.inf, m[c])
            m_new.append(jnp.maximum(m_prev, cur_max[c]))
            acc_ref[c] = acc_ref[c] * jnp.exp2(m_prev - m_new[c])
        m_new = tuple(m_new)

        def body(ci, part_max):
            r = ci * chunk
            part_max = score_chunk(next_q_start, next_k_start, r, s_next, part_max)
            softmax_chunk(k_start, r, s_cur, m_new)
            return part_max

        part_max = lax.fori_loop(0, tk // chunk, body, (neg_inf_rows, neg_inf_rows))
        return m_new, block_max(part_max)

    lam = (jnp.exp(jnp.sum(lq1_ref[...] * lk1_ref[...], axis=-1, keepdims=True))
           - jnp.exp(jnp.sum(lq2_ref[...] * lk2_ref[...], axis=-1, keepdims=True))
           + lam_init)

    def finalize(q_tile):
        acc0 = acc_ref[0]
        acc1 = acc_ref[1]
        o_t = (acc0[:DA_V_DIM] * (1.0 / acc0[DA_V_DIM:DA_V_DIM + 1])
               - lam * (acc1[:DA_V_DIM] * (1.0 / acc1[DA_V_DIM:DA_V_DIM + 1])))
        ms = jnp.mean(o_t * o_t, axis=0, keepdims=True)
        o = o_t * lax.rsqrt(ms + NORM_EPS) * g_ref[...] * (1.0 - lam_init)
        o_ref[:, pl.ds(pl.multiple_of(q_tile * tq, tq), tq)] = o.astype(BF16)

    acc_ref[...] = jnp.zeros(acc_ref.shape, F32)
    first_max = lax.fori_loop(
        0, tk // chunk,
        lambda ci, pm: score_chunk(0, 0, ci * chunk, s_even, pm),
        (neg_inf_rows, neg_inf_rows))
    neg_inf = jnp.full((1, tq), -jnp.inf, F32)

    def pair(i, carry):
        m, max_even = carry
        t = 2 * i
        m, max_odd = step(t, t + 1, s_even, s_odd, m, max_even)
        m, max_even = step(t + 1, jnp.where(t + 2 == n_steps, 0, t + 2), s_odd, s_even, m, max_odd)

        @pl.when((t + 1) % n_blk == n_blk - 1)
        def _():
            finalize((t + 1) // n_blk)

        return m, max_even

    lax.fori_loop(0, n_steps // 2, pair, ((neg_inf, neg_inf), block_max(first_max)))


def _attention(q8, k8, vt, lq1, lk1, lq2, lk2, g, layer, lam_init):
    batch, _, _, seq = q8.shape
    k8 = k8.reshape(batch, seq, DA_HEADS * 2 * QK_STACK)
    tq = ATTN_Q_TILE
    tk = min(ATTN_K_TILE, seq // 2)
    chunk = min(ATTN_ROW_CHUNK, tk)
    assert seq % tq == 0 and seq % (2 * tk) == 0 and tk % chunk == 0
    small = lambda width: _resident((None, 1, width), lambda b, h: (layer, 0, 0))
    g = g.reshape(g.shape[0], DA_V_DIM, 1)
    return pl.pallas_call(
        functools.partial(_attn_kernel, lam_init=lam_init, seq=seq, tq=tq, tk=tk, chunk=chunk),
        grid=(batch, DA_HEADS),
        in_specs=[
            pl.BlockSpec((None, None, 2 * QK_STACK, seq), lambda b, h: (b, h, 0, 0)),
            pl.BlockSpec((None, seq, 2 * QK_STACK), lambda b, h: (b, 0, h)),
            pl.BlockSpec((None, None, VT_ROWS, seq), lambda b, h: (b, h, 0, 0)),
            small(DA_HEAD_DIM), small(DA_HEAD_DIM), small(DA_HEAD_DIM), small(DA_HEAD_DIM),
            _resident((None, DA_V_DIM, 1), lambda b, h: (layer, 0, 0)),
        ],
        out_specs=pl.BlockSpec((None, DA_V_DIM, seq), lambda b, h: (b, h, 0)),
        out_shape=jax.ShapeDtypeStruct((batch, DA_WIDTH, seq), BF16),
        scratch_shapes=[
            pltpu.VMEM((2, tk, tq), F32),
            pltpu.VMEM((2, tk, tq), F32),
            pltpu.VMEM((2, VT_ROWS, tq), F32),
        ],
        compiler_params=_compiler_params(2),
        name="diff_attention",
    )(q8, k8, vt, lq1, lk1, lq2, lk2, g)


def _mem_kv_kernel(mem_ref, g_ref, w_ref, kt_ref, v_ref):
    hm = _rms_norm(mem_ref[...], g_ref[...]).astype(BF16)
    kv = _dot(hm, w_ref[...])
    k_t = kv[:, :CA_WIDTH].T
    v = kv[:, CA_WIDTH:]
    k_head = lax.broadcasted_iota(jnp.int32, k_t.shape, 0) // CA_HEAD_DIM
    v_head = lax.broadcasted_iota(jnp.int32, v.shape, 1) // CA_HEAD_DIM
    for h in range(CA_HEADS):
        kt_ref[h] = jnp.where(k_head == h, k_t, 0.0).astype(BF16)
        v_ref[h] = jnp.where(v_head == h, v, 0.0).astype(BF16)


def _mem_kv(mem, g, w):
    batch, n_mem, d = mem.shape
    depth = w.shape[0]
    out_spec = lambda r, c: pl.BlockSpec((None, None, CA_HEADS, r, c), lambda l, b: (l, b, 0, 0, 0))
    return pl.pallas_call(
        _mem_kv_kernel,
        grid=(depth, batch),
        in_specs=[
            pl.BlockSpec((None, n_mem, d), lambda l, b: (b, 0, 0)),
            pl.BlockSpec((None, 1, d), lambda l, b: (l, 0, 0)),
            pl.BlockSpec((None, d, 2 * CA_WIDTH), lambda l, b: (l, 0, 0)),
        ],
        out_specs=[out_spec(CA_WIDTH, n_mem), out_spec(n_mem, CA_WIDTH)],
        out_shape=[
            jax.ShapeDtypeStruct((depth, batch, CA_HEADS, CA_WIDTH, n_mem), BF16),
            jax.ShapeDtypeStruct((depth, batch, CA_HEADS, n_mem, CA_WIDTH), BF16),
        ],
        compiler_params=_compiler_params(2),
        name="mem_kv",
    )(mem, g, w)


def _pooled(u_prev_ref, u_ref, u_next_ref, ext_ref, seq):
    tm = u_ref.shape[0]
    tiles_per_seq = seq // tm
    t = pl.program_id(0) % tiles_per_seq
    u = u_ref[...]
    ext_ref[pl.ds(0, POOL_HALO), :] = jnp.where(t > 0, u_prev_ref[...], 0.0)
    ext_ref[pl.ds(POOL_HALO, tm), :] = u
    ext_ref[pl.ds(POOL_HALO + tm, POOL_HALO), :] = jnp.where(t < tiles_per_seq - 1, u_next_ref[...], 0.0)

    def window(lo, hi):
        acc = ext_ref[pl.ds(POOL_HALO + lo, tm), :]
        for d in range(lo + 1, hi):
            acc = acc + ext_ref[pl.ds(POOL_HALO + d, tm), :]
        return acc

    pos = t * tm + lax.broadcasted_iota(jnp.int32, u.shape, 0)
    group = lax.broadcasted_iota(jnp.int32, u.shape, 1) // POOL_GROUP_DIM
    total = jnp.zeros_like(u)
    count = jnp.ones_like(u)
    for gi, w in enumerate(POOL_WINDOWS):
        half = w // 2
        cnt = (jnp.minimum(pos + half, seq) - jnp.maximum(pos - half, 0)).astype(F32)
        total = jnp.where(group == gi, window(-half, half), total)
        count = jnp.where(group == gi, cnt, count)
    return total / count - u


def _merge_kernel(x_ref, u_prev_ref, u_ref, u_next_ref, o_da_ref, qca_ref, kt_ref, v_ref,
                  g_pre_ref, w_gate_ref, b_gate_ref, pool_w_ref, pool_scale_ref,
                  w_pool_ref, w_da_ref, w_ca_ref, w_out_ref, g_post_ref,
                  out_ref, ext_ref, *, seq):
    x = x_ref[...]
    d = x.shape[-1]
    h = _rms_norm(x, g_pre_ref[...]).astype(BF16)
    gates = jax.nn.sigmoid(_dot(h, w_gate_ref[...]) + b_gate_ref[...])

    pooled = _pooled(u_prev_ref, u_ref, u_next_ref, ext_ref, seq).astype(BF16)
    y_pool = (_dot(pooled, pool_w_ref[...]) * pool_scale_ref[...]).astype(BF16)

    qca = qca_ref[...]
    y_ca = jnp.zeros(qca.shape, F32)
    for hd in range(CA_HEADS):
        s = _dot(qca, kt_ref[hd])
        p = jnp.exp2(s - jnp.max(s, axis=-1, keepdims=True))
        p = p * (1.0 / jnp.sum(p, axis=-1, keepdims=True))
        y_ca = y_ca + _dot(p.astype(BF16), v_ref[hd])

    merged = (gates[:, :d] * _dot(y_pool, w_pool_ref[...])
              + gates[:, d:2 * d] * _dot_t(o_da_ref[...], w_da_ref[...])
              + gates[:, 2 * d:] * _dot(y_ca.astype(BF16), w_ca_ref[...]))
    y = _dot(merged.astype(BF16), w_out_ref[...])
    out_ref[...] = x + _rms_norm(y, g_post_ref[...])


def _merge(x, u, o_da, qca, kt, v, g_pre, w_gate, b_gate, pool_w, pool_scale,
           w_pool, w_da, w_ca, w_out, g_post, layer, seq):
    n, d = x.shape
    n_mem = v.shape[-2]
    tm = min(WIDE_TILE, seq)
    tiles_per_seq = seq // tm
    halo_blocks = tm // POOL_HALO
    n_halo = n // POOL_HALO
    row = lambda width: pl.BlockSpec((tm, width), lambda i: (i, 0))
    layer_block = lambda *shape: _resident((None,) + shape, lambda i: (layer,) + (0,) * len(shape))
    mem_block = lambda r, c: pl.BlockSpec((None, None, CA_HEADS, r, c),
                                          lambda i: (layer, i // tiles_per_seq, 0, 0, 0))
    return pl.pallas_call(
        functools.partial(_merge_kernel, seq=seq),
        grid=(n // tm,),
        in_specs=[
            row(d),
            pl.BlockSpec((POOL_HALO, POOL_WIDTH), lambda i: (jnp.maximum(i * halo_blocks - 1, 0), 0)),
            row(POOL_WIDTH),
            pl.BlockSpec((POOL_HALO, POOL_WIDTH), lambda i: (jnp.minimum((i + 1) * halo_blocks, n_halo - 1), 0)),
            pl.BlockSpec((None, DA_WIDTH, tm), lambda i: (i // tiles_per_seq, 0, i % tiles_per_seq)),
            row(CA_WIDTH),
            mem_block(CA_WIDTH, n_mem),
            mem_block(n_mem, CA_WIDTH),
            layer_block(1, d),
            layer_block(d, 3 * d),
            layer_block(1, 3 * d),
            layer_block(POOL_WIDTH, POOL_WIDTH),
            layer_block(1, POOL_WIDTH),
            layer_block(POOL_WIDTH, d),
            layer_block(DA_WIDTH, d),
            layer_block(CA_WIDTH, d),
            layer_block(d, d),
            layer_block(1, d),
        ],
        out_specs=row(d),
        out_shape=jax.ShapeDtypeStruct((n, d), F32),
        scratch_shapes=[pltpu.VMEM((tm + 2 * POOL_HALO, POOL_WIDTH), F32)],
        compiler_params=_compiler_params(1),
        name="merge",
    )(x, u, u, u, o_da, qca, kt, v, g_pre, w_gate, b_gate, pool_w, pool_scale,
      w_pool, w_da, w_ca, w_out, g_post)


def _block_diag(w):
    depth, groups, c, _ = w.shape
    eye = jnp.eye(groups, dtype=w.dtype)
    return jnp.einsum("lgcd,gh->lgchd", w, eye).reshape(depth, groups * c, groups * c)


def kernel(x, mem, positions, ffn1_pre_g, ffn1_w_up, ffn1_w_down, ffn1_post_g, mix_pre_g, w_in, pool_w, pool_scale, da_lambda_q1, da_lambda_k1, da_lambda_q2, da_lambda_k2, da_subln_g, mem_norm_g, w_mem_kv, w_gate, b_gate, w_br_pool, w_br_da, w_br_ca, w_out, mix_post_g, ffn2_pre_g, ffn2_w_up, ffn2_w_down, ffn2_post_g):
    batch, seq, d = x.shape
    depth = w_in.shape[0]
    assert seq % TOKEN_TILE == 0 and seq % min(WIDE_TILE, seq) == 0 and seq % ATTN_Q_TILE == 0
    assert w_in.shape[-1] == POOL_WIDTH + 3 * DA_WIDTH + CA_WIDTH

    vec = lambda a: a.astype(F32).reshape(depth, 1, a.shape[-1])
    bf = lambda a: a.astype(BF16)

    cos_t, sin_t = _rope_tables(positions)
    kt_all, v_all = _mem_kv(mem, vec(mem_norm_g), bf(w_mem_kv))

    ffn1 = (vec(ffn1_pre_g), bf(ffn1_w_up), bf(ffn1_w_down), vec(ffn1_post_g))
    ffn2 = (vec(ffn2_pre_g), bf(ffn2_w_up), bf(ffn2_w_down), vec(ffn2_post_g))
    mix_g, w_in_b = vec(mix_pre_g), bf(w_in)
    lam_vecs = tuple(vec(a) for a in (da_lambda_q1, da_lambda_k1, da_lambda_q2, da_lambda_k2))
    merge_params = (mix_g, bf(w_gate), vec(b_gate), bf(_block_diag(pool_w)), vec(pool_scale),
                    bf(w_br_pool), bf(w_br_da), bf(w_br_ca), bf(w_out), vec(mix_post_g))
    subln_g = vec(da_subln_g)

    xt = x.reshape(batch * seq, d)
    for layer in range(depth):
        lam_init = 0.8 - 0.6 * math.exp(-0.3 * layer)
        xt = _ffn(xt, *ffn1, layer)
        u, q8, k8, vt, qca = _inproj(xt, mix_g, w_in_b, cos_t, sin_t, layer, batch, seq)
        o_da = _attention(q8, k8, vt, *lam_vecs, subln_g, layer, lam_init)
        xt = _merge(xt, u, o_da, qca, kt_all, v_all, *merge_params, layer, seq)
        xt = _ffn(xt, *ffn2, layer)
    return xt.reshape(batch, seq, d)
```

```python
import functools
import math

import jax
import jax.numpy as jnp
import numpy as np
from jax import lax
from jax.experimental import pallas as pl
from jax.experimental.pallas import tpu as pltpu

F32 = jnp.float32
BF16 = jnp.bfloat16
F8 = jnp.float8_e4m3fn

POOL_WINDOWS = (2, 4, 8, 16)
POOL_WIDTH = 256
POOL_GROUP_DIM = POOL_WIDTH // len(POOL_WINDOWS)
POOL_HALO = max(POOL_WINDOWS) // 2
DA_HEADS = 4
DA_HEAD_DIM = 64
DA_V_DIM = 2 * DA_HEAD_DIM
DA_WIDTH = DA_HEADS * DA_V_DIM
CA_HEADS = 4
CA_HEAD_DIM = 64
CA_WIDTH = CA_HEADS * CA_HEAD_DIM
ROPE_THETA = 500000.0
ROPE_DIM = DA_HEAD_DIM // 4
ROPE_HALF = ROPE_DIM // 2
NORM_EPS = 1e-6
LOG2E = math.log2(math.e)

V7X_LANES = 128
V7X_SUBLANES = 8
V7X_VMEM_LIMIT_BYTES = 60000 * 1024

TOKEN_TILE = 512
WIDE_TILE = 1024
ATTN_Q_TILE = 512
ATTN_K_TILE = 2048
ATTN_ROW_CHUNK = 1024
BF16_SUBLANE_TILE = 2 * V7X_SUBLANES
VT_ROWS = DA_V_DIM + BF16_SUBLANE_TILE
V7X_MXU_DEPTH = 256
QK_STACK = V7X_MXU_DEPTH


def _compiler_params(n_grid_axes):
    return pltpu.CompilerParams(
        dimension_semantics=("arbitrary",) * n_grid_axes,
        vmem_limit_bytes=V7X_VMEM_LIMIT_BYTES,
    )


def _resident(block_shape, index_map):
    return pl.BlockSpec(block_shape, index_map, pipeline_mode=pl.Buffered(1))


def _rms_norm(x, g):
    ms = jnp.mean(x * x, axis=-1, keepdims=True)
    return x * lax.rsqrt(ms + NORM_EPS) * g


def _dot(a, b):
    return jnp.dot(a, b, preferred_element_type=F32)


def _dot_t(a_t, b):
    return lax.dot_general(a_t, b, (((0,), (0,)), ((), ())), preferred_element_type=F32)


def _rope_table_kernel(pos_ref, cos_ref, sin_ref):
    pos = pos_ref[...].astype(F32)
    lane = lax.broadcasted_iota(jnp.int32, pos.shape, 1)
    in_comp = lane % DA_HEAD_DIM
    freq_idx = lane % ROPE_HALF
    inv_freq = (np.float32(ROPE_THETA) ** (-np.arange(0, ROPE_DIM, 2, dtype=np.float32) / np.float32(ROPE_DIM)))
    inv = jnp.full(pos.shape, float(inv_freq[0]), F32)
    for j in range(1, ROPE_HALF):
        inv = jnp.where(freq_idx == j, float(inv_freq[j]), inv)
    ang = pos * inv
    rotary = in_comp < ROPE_DIM
    cos_ref[...] = jnp.where(rotary, jnp.cos(ang), 1.0)
    sin_v = jnp.sin(ang)
    sin_ref[...] = jnp.where(rotary, jnp.where(in_comp < ROPE_HALF, -sin_v, sin_v), 0.0)


def _rope_tables(positions):
    n = positions.size
    pos = jnp.broadcast_to(positions.reshape(n, 1), (n, V7X_LANES))
    spec = pl.BlockSpec((TOKEN_TILE, V7X_LANES), lambda i: (i, 0))
    return pl.pallas_call(
        _rope_table_kernel,
        grid=(n // TOKEN_TILE,),
        in_specs=[spec],
        out_specs=[spec, spec],
        out_shape=[jax.ShapeDtypeStruct((n, V7X_LANES), F32)] * 2,
        compiler_params=_compiler_params(1),
        name="rope_tables",
    )(pos)


def _ffn_kernel(x_ref, g_pre_ref, w_up_ref, w_down_ref, g_post_ref, o_ref, *, d_ff):
    x = x_ref[...]
    h = _rms_norm(x, g_pre_ref[...]).astype(BF16)
    ab = _dot(h, w_up_ref[...])
    a = ab[:, :d_ff]
    b = ab[:, d_ff:]
    act = (a * jax.nn.sigmoid(a) * b).astype(BF16)
    y = _dot(act, w_down_ref[...])
    o_ref[...] = x + 0.5 * _rms_norm(y, g_post_ref[...])


def _ffn(x, g_pre, w_up, w_down, g_post, layer):
    n, d = x.shape
    d_ff = w_down.shape[1]
    row = pl.BlockSpec((TOKEN_TILE, d), lambda i: (i, 0))
    gain = _resident((None, 1, d), lambda i: (layer, 0, 0))
    return pl.pallas_call(
        functools.partial(_ffn_kernel, d_ff=d_ff),
        grid=(n // TOKEN_TILE,),
        in_specs=[
            row,
            gain,
            _resident((None, d, 2 * d_ff), lambda i: (layer, 0, 0)),
            _resident((None, d_ff, d), lambda i: (layer, 0, 0)),
            gain,
        ],
        out_specs=row,
        out_shape=jax.ShapeDtypeStruct((n, d), F32),
        compiler_params=_compiler_params(1),
        name="ffn",
    )(x, g_pre, w_up, w_down, g_post)


def _apply_rope(t, cos_t, sin_t):
    width = t.shape[-1]
    lane = lax.broadcasted_iota(jnp.int32, t.shape, 1) % DA_HEAD_DIM
    partner = jnp.where(lane < ROPE_HALF,
                        pltpu.roll(t, width - ROPE_HALF, 1),
                        pltpu.roll(t, ROPE_HALF, 1))
    reps = width // cos_t.shape[-1]
    cos_w = jnp.concatenate([cos_t] * reps, axis=-1)
    sin_w = jnp.concatenate([sin_t] * reps, axis=-1)
    return t * cos_w + partner * sin_w


def _split_fp8(x):
    fp8_max = float(jnp.finfo(F8).max)
    x = jnp.clip(x, -fp8_max, fp8_max)
    hi = x.astype(F8)
    return hi, x - hi.astype(F32)


def _inproj_kernel(x_ref, g_ref, w_ref, cos_ref, sin_ref,
                   u_ref, q8_ref, k8_ref, vt_ref, qca_ref):
    h = _rms_norm(x_ref[...], g_ref[...]).astype(BF16)
    proj = _dot(h, w_ref[...])
    tm = proj.shape[0]
    o_q = POOL_WIDTH
    o_k = o_q + DA_WIDTH
    o_v = o_k + DA_WIDTH
    o_ca = o_v + DA_WIDTH
    cos_t = cos_ref[...]
    sin_t = sin_ref[...]
    u_ref[...] = proj[:, :o_q]
    qk_scale = math.sqrt(DA_HEAD_DIM ** -0.5 * LOG2E)
    q_t = (_apply_rope(proj[:, o_q:o_k], cos_t, sin_t) * qk_scale).T
    k = _apply_rope(proj[:, o_k:o_v], cos_t, sin_t) * qk_scale

    first_half = lax.broadcasted_iota(jnp.int32, (tm, DA_V_DIM), 1) < DA_HEAD_DIM
    zero_rows = jnp.zeros((DA_HEAD_DIM, tm), F8)
    for hd in range(DA_HEADS):
        k_head = k[:, hd * DA_V_DIM:(hd + 1) * DA_V_DIM]
        k_swap = pltpu.roll(k_head, DA_HEAD_DIM, 1)
        for c in range(2):
            both = jnp.where(first_half, k_head, k_swap) if c == 0 else jnp.where(first_half, k_swap, k_head)
            hi, lo = _split_fp8(both)
            col = (2 * hd + c) * QK_STACK
            k8_ref[:, col:col + DA_V_DIM] = hi
            k8_ref[:, col + DA_V_DIM:col + QK_STACK] = jnp.where(first_half, lo, 0.0).astype(F8)
            row = hd * DA_V_DIM + c * DA_HEAD_DIM
            hi, lo = _split_fp8(q_t[row:row + DA_HEAD_DIM])
            base = c * QK_STACK
            q8_ref[hd, base:base + DA_HEAD_DIM] = hi
            q8_ref[hd, base + DA_HEAD_DIM:base + 2 * DA_HEAD_DIM] = lo.astype(F8)
            q8_ref[hd, base + 2 * DA_HEAD_DIM:base + 3 * DA_HEAD_DIM] = hi
            q8_ref[hd, base + 3 * DA_HEAD_DIM:base + QK_STACK] = zero_rows
    vt = proj[:, o_v:o_ca].T.astype(BF16)
    ones_row = (lax.broadcasted_iota(jnp.int32, (BF16_SUBLANE_TILE, vt.shape[1]), 0) == 0).astype(BF16)
    for hd in range(DA_HEADS):
        vt_ref[hd, :DA_V_DIM, :] = vt[hd * DA_V_DIM:(hd + 1) * DA_V_DIM]
        vt_ref[hd, DA_V_DIM:, :] = ones_row
    qca_ref[...] = (proj[:, o_ca:] * (CA_HEAD_DIM ** -0.5 * LOG2E)).astype(BF16)


def _inproj(x, g, w_in, cos_t, sin_t, layer, batch, seq):
    n, d = x.shape
    tm = min(WIDE_TILE, seq)
    tiles_per_seq = seq // tm
    row = lambda width: pl.BlockSpec((tm, width), lambda i: (i, 0))
    col = lambda rows: pl.BlockSpec((None, DA_HEADS, rows, tm),
                                    lambda i: (i // tiles_per_seq, 0, 0, i % tiles_per_seq))
    return pl.pallas_call(
        _inproj_kernel,
        grid=(n // tm,),
        in_specs=[
            row(d),
            _resident((None, 1, d), lambda i: (layer, 0, 0)),
            _resident((None, d, w_in.shape[-1]), lambda i: (layer, 0, 0)),
            row(V7X_LANES),
            row(V7X_LANES),
        ],
        out_specs=[row(POOL_WIDTH), col(2 * QK_STACK), row(DA_HEADS * 2 * QK_STACK), col(VT_ROWS), row(CA_WIDTH)],
        out_shape=[
            jax.ShapeDtypeStruct((n, POOL_WIDTH), F32),
            jax.ShapeDtypeStruct((batch, DA_HEADS, 2 * QK_STACK, seq), F8),
            jax.ShapeDtypeStruct((n, DA_HEADS * 2 * QK_STACK), F8),
            jax.ShapeDtypeStruct((batch, DA_HEADS, VT_ROWS, seq), BF16),
            jax.ShapeDtypeStruct((n, CA_WIDTH), BF16),
        ],
        compiler_params=_compiler_params(1),
        name="inproj",
    )(x, g, w_in, cos_t, sin_t)


def _attn_kernel(q8_ref, k8_ref, vt_ref, lq1_ref, lk1_ref, lq2_ref, lk2_ref, g_ref, o_ref,
                 s_even, s_odd, acc_ref, *, lam_init, seq, tq, tk, chunk):
    n_blk = seq // tk
    n_steps = (seq // tq) * n_blk
    neg_inf_rows = jnp.full((V7X_SUBLANES, tq), -jnp.inf, F32)

    def score_chunk(q_start, k_start, r, s_buf, part_max):
        rows = pl.ds(pl.multiple_of(r, chunk), chunk)
        out = []
        for c in range(2):
            cols = slice(c * QK_STACK, (c + 1) * QK_STACK)
            s = _dot(k8_ref[pl.ds(pl.multiple_of(k_start + r, chunk), chunk), cols],
                     q8_ref[cols, pl.ds(q_start, tq)])
            s_buf[c, rows, :] = s
            s_max = jnp.max(s.reshape(-1, V7X_SUBLANES, tq), axis=0)
            out.append(jnp.maximum(part_max[c], s_max))
        return tuple(out)

    def softmax_chunk(k_start, r, s_buf, m_new, rescale):
        rows = pl.ds(pl.multiple_of(r, chunk), chunk)
        vt_rows = vt_ref[:, pl.ds(pl.multiple_of(k_start + r, chunk), chunk)]
        for c in range(2):
            p = jnp.exp2((s_buf[c, rows, :] - m_new[c]).astype(BF16))
            acc_ref[c] = acc_ref[c] * rescale[c] + _dot(vt_rows, p)

    def block_max(part_max):
        return tuple(jnp.max(pm, axis=0, keepdims=True) for pm in part_max)

    def step(t, t_next, s_cur, s_next, m, cur_max):
        k_blk = t % n_blk
        k_start = k_blk * tk
        next_q_start = pl.multiple_of((t_next // n_blk) * tq, tq)
        next_k_start = (t_next % n_blk) * tk
        m_new, alpha = [], []
        for c in range(2):
            m_prev = jnp.where(k_blk == 0, -jnp.inf, m[c])
            m_new.append(jnp.maximum(m_prev, cur_max[c]))
            alpha.append(jnp.exp2(m_prev - m_new[c]))
        m_new = tuple(m_new)

        def body(ci, part_max):
            r = ci * chunk
            part_max = score_chunk(next_q_start, next_k_start, r, s_next, part_max)
            softmax_chunk(k_start, r, s_cur, m_new, [jnp.where(ci == 0, a, 1.0) for a in alpha])
            return part_max

        part_max = lax.fori_loop(0, tk // chunk, body, (neg_inf_rows, neg_inf_rows))
        return m_new, block_max(part_max)

    lam = (jnp.exp(jnp.sum(lq1_ref[...] * lk1_ref[...], axis=-1, keepdims=True))
           - jnp.exp(jnp.sum(lq2_ref[...] * lk2_ref[...], axis=-1, keepdims=True))
           + lam_init)

    def finalize(q_tile):
        acc0 = acc_ref[0]
        acc1 = acc_ref[1]
        o_t = (acc0[:DA_V_DIM] * (1.0 / acc0[DA_V_DIM:DA_V_DIM + 1])
               - lam * (acc1[:DA_V_DIM] * (1.0 / acc1[DA_V_DIM:DA_V_DIM + 1])))
        ms = jnp.mean(o_t * o_t, axis=0, keepdims=True)
        o = o_t * lax.rsqrt(ms + NORM_EPS) * g_ref[...] * (1.0 - lam_init)
        o_ref[:, pl.ds(pl.multiple_of(q_tile * tq, tq), tq)] = o.astype(BF16)

    acc_ref[...] = jnp.zeros(acc_ref.shape, F32)
    first_max = lax.fori_loop(
        0, tk // chunk,
        lambda ci, pm: score_chunk(0, 0, ci * chunk, s_even, pm),
        (neg_inf_rows, neg_inf_rows))
    neg_inf = jnp.full((1, tq), -jnp.inf, F32)

    def pair(i, carry):
        m, max_even = carry
        t = 2 * i
        m, max_odd = step(t, t + 1, s_even, s_odd, m, max_even)
        m, max_even = step(t + 1, jnp.where(t + 2 == n_steps, 0, t + 2), s_odd, s_even, m, max_odd)

        @pl.when((t + 1) % n_blk == n_blk - 1)
        def _():
            finalize((t + 1) // n_blk)

        return m, max_even

    lax.fori_loop(0, n_steps // 2, pair, ((neg_inf, neg_inf), block_max(first_max)))


def _attention(q8, k8, vt, lq1, lk1, lq2, lk2, g, layer, lam_init):
    batch, _, _, seq = q8.shape
    k8 = k8.reshape(batch, seq, DA_HEADS * 2 * QK_STACK)
    tq = ATTN_Q_TILE
    tk = min(ATTN_K_TILE, seq // 2)
    chunk = min(ATTN_ROW_CHUNK, tk)
    assert seq % tq == 0 and seq % (2 * tk) == 0 and tk % chunk == 0
    small = lambda width: _resident((None, 1, width), lambda b, h: (layer, 0, 0))
    g = g.reshape(g.shape[0], DA_V_DIM, 1)
    return pl.pallas_call(
        functools.partial(_attn_kernel, lam_init=lam_init, seq=seq, tq=tq, tk=tk, chunk=chunk),
        grid=(batch, DA_HEADS),
        in_specs=[
            pl.BlockSpec((None, None, 2 * QK_STACK, seq), lambda b, h: (b, h, 0, 0)),
            pl.BlockSpec((None, seq, 2 * QK_STACK), lambda b, h: (b, 0, h)),
            pl.BlockSpec((None, None, VT_ROWS, seq), lambda b, h: (b, h, 0, 0)),
            small(DA_HEAD_DIM), small(DA_HEAD_DIM), small(DA_HEAD_DIM), small(DA_HEAD_DIM),
            _resident((None, DA_V_DIM, 1), lambda b, h: (layer, 0, 0)),
        ],
        out_specs=pl.BlockSpec((None, DA_V_DIM, seq), lambda b, h: (b, h, 0)),
        out_shape=jax.ShapeDtypeStruct((batch, DA_WIDTH, seq), BF16),
        scratch_shapes=[
            pltpu.VMEM((2, tk, tq), F32),
            pltpu.VMEM((2, tk, tq), F32),
            pltpu.VMEM((2, VT_ROWS, tq), F32),
        ],
        compiler_params=_compiler_params(2),
        name="diff_attention",
    )(q8, k8, vt, lq1, lk1, lq2, lk2, g)


def _mem_kv_kernel(mem_ref, g_ref, w_ref, kt_ref, v_ref):
    hm = _rms_norm(mem_ref[...], g_ref[...]).astype(BF16)
    kv = _dot(hm, w_ref[...])
    k_t = kv[:, :CA_WIDTH].T
    v = kv[:, CA_WIDTH:]
    k_head = lax.broadcasted_iota(jnp.int32, k_t.shape, 0) // CA_HEAD_DIM
    v_head = lax.broadcasted_iota(jnp.int32, v.shape, 1) // CA_HEAD_DIM
    for h in range(CA_HEADS):
        kt_ref[h] = jnp.where(k_head == h, k_t, 0.0).astype(BF16)
        v_ref[h] = jnp.where(v_head == h, v, 0.0).astype(BF16)


def _mem_kv(mem, g, w):
    batch, n_mem, d = mem.shape
    depth = w.shape[0]
    out_spec = lambda r, c: pl.BlockSpec((None, None, CA_HEADS, r, c), lambda l, b: (l, b, 0, 0, 0))
    return pl.pallas_call(
        _mem_kv_kernel,
        grid=(depth, batch),
        in_specs=[
            pl.BlockSpec((None, n_mem, d), lambda l, b: (b, 0, 0)),
            pl.BlockSpec((None, 1, d), lambda l, b: (l, 0, 0)),
            pl.BlockSpec((None, d, 2 * CA_WIDTH), lambda l, b: (l, 0, 0)),
        ],
        out_specs=[out_spec(CA_WIDTH, n_mem), out_spec(n_mem, CA_WIDTH)],
        out_shape=[
            jax.ShapeDtypeStruct((depth, batch, CA_HEADS, CA_WIDTH, n_mem), BF16),
            jax.ShapeDtypeStruct((depth, batch, CA_HEADS, n_mem, CA_WIDTH), BF16),
        ],
        compiler_params=_compiler_params(2),
        name="mem_kv",
    )(mem, g, w)


def _pooled(u_prev_ref, u_ref, u_next_ref, ext_ref, seq):
    tm = u_ref.shape[0]
    tiles_per_seq = seq // tm
    t = pl.program_id(0) % tiles_per_seq
    u = u_ref[...]
    ext_ref[pl.ds(0, POOL_HALO), :] = jnp.where(t > 0, u_prev_ref[...], 0.0)
    ext_ref[pl.ds(POOL_HALO, tm), :] = u
    ext_ref[pl.ds(POOL_HALO + tm, POOL_HALO), :] = jnp.where(t < tiles_per_seq - 1, u_next_ref[...], 0.0)

    def window(lo, hi, lanes):
        acc = ext_ref[pl.ds(POOL_HALO + lo, tm), lanes]
        for d in range(lo + 1, hi):
            acc = acc + ext_ref[pl.ds(POOL_HALO + d, tm), lanes]
        return acc

    groups_per_tile = V7X_LANES // POOL_GROUP_DIM
    shape = (tm, V7X_LANES)
    pos = t * tm + lax.broadcasted_iota(jnp.int32, shape, 0)
    group_in_tile = lax.broadcasted_iota(jnp.int32, shape, 1) // POOL_GROUP_DIM
    tiles = []
    for tile in range(POOL_WIDTH // V7X_LANES):
        lanes = slice(tile * V7X_LANES, (tile + 1) * V7X_LANES)
        total = jnp.zeros(shape, F32)
        count = jnp.ones(shape, F32)
        for gi in range(groups_per_tile):
            half = POOL_WINDOWS[tile * groups_per_tile + gi] // 2
            cnt = (jnp.minimum(pos + half, seq) - jnp.maximum(pos - half, 0)).astype(F32)
            total = jnp.where(group_in_tile == gi, window(-half, half, lanes), total)
            count = jnp.where(group_in_tile == gi, cnt, count)
        tiles.append(total / count)
    return jnp.concatenate(tiles, axis=-1) - u


def _merge_kernel(x_ref, u_prev_ref, u_ref, u_next_ref, o_da_ref, qca_ref, kt_ref, v_ref,
                  g_pre_ref, w_gate_ref, b_gate_ref, pool_w_ref, pool_scale_ref,
                  w_pool_ref, w_da_ref, w_ca_ref, w_out_ref, g_post_ref,
                  out_ref, ext_ref, *, seq):
    x = x_ref[...]
    d = x.shape[-1]
    h = _rms_norm(x, g_pre_ref[...]).astype(BF16)
    gates = jax.nn.sigmoid(_dot(h, w_gate_ref[...]) + b_gate_ref[...])

    pooled = _pooled(u_prev_ref, u_ref, u_next_ref, ext_ref, seq).astype(BF16)
    y_pool = (_dot(pooled, pool_w_ref[...]) * pool_scale_ref[...]).astype(BF16)

    qca = qca_ref[...]
    y_ca = jnp.zeros(qca.shape, F32)
    for hd in range(CA_HEADS):
        s = _dot(qca, kt_ref[hd])
        p = jnp.exp2(s - jnp.max(s, axis=-1, keepdims=True))
        p = p * (1.0 / jnp.sum(p, axis=-1, keepdims=True))
        y_ca = y_ca + _dot(p.astype(BF16), v_ref[hd])

    merged = (gates[:, :d] * _dot(y_pool, w_pool_ref[...])
              + gates[:, d:2 * d] * _dot_t(o_da_ref[...], w_da_ref[...])
              + gates[:, 2 * d:] * _dot(y_ca.astype(BF16), w_ca_ref[...]))
    y = _dot(merged.astype(BF16), w_out_ref[...])
    out_ref[...] = x + _rms_norm(y, g_post_ref[...])


def _merge(x, u, o_da, qca, kt, v, g_pre, w_gate, b_gate, pool_w, pool_scale,
           w_pool, w_da, w_ca, w_out, g_post, layer, seq):
    n, d = x.shape
    n_mem = v.shape[-2]
    tm = min(WIDE_TILE, seq)
    tiles_per_seq = seq // tm
    halo_blocks = tm // POOL_HALO
    n_halo = n // POOL_HALO
    row = lambda width: pl.BlockSpec((tm, width), lambda i: (i, 0))
    layer_block = lambda *shape: _resident((None,) + shape, lambda i: (layer,) + (0,) * len(shape))
    mem_block = lambda r, c: pl.BlockSpec((None, None, CA_HEADS, r, c),
                                          lambda i: (layer, i // tiles_per_seq, 0, 0, 0))
    return pl.pallas_call(
        functools.partial(_merge_kernel, seq=seq),
        grid=(n // tm,),
        in_specs=[
            row(d),
            pl.BlockSpec((POOL_HALO, POOL_WIDTH), lambda i: (jnp.maximum(i * halo_blocks - 1, 0), 0)),
            row(POOL_WIDTH),
            pl.BlockSpec((POOL_HALO, POOL_WIDTH), lambda i: (jnp.minimum((i + 1) * halo_blocks, n_halo - 1), 0)),
            pl.BlockSpec((None, DA_WIDTH, tm), lambda i: (i // tiles_per_seq, 0, i % tiles_per_seq)),
            row(CA_WIDTH),
            mem_block(CA_WIDTH, n_mem),
            mem_block(n_mem, CA_WIDTH),
            layer_block(1, d),
            layer_block(d, 3 * d),
            layer_block(1, 3 * d),
            layer_block(POOL_WIDTH, POOL_WIDTH),
            layer_block(1, POOL_WIDTH),
            layer_block(POOL_WIDTH, d),
            layer_block(DA_WIDTH, d),
            layer_block(CA_WIDTH, d),
            layer_block(d, d),
            layer_block(1, d),
        ],
        out_specs=row(d),
        out_shape=jax.ShapeDtypeStruct((n, d), F32),
        scratch_shapes=[pltpu.VMEM((tm + 2 * POOL_HALO, POOL_WIDTH), F32)],
        compiler_params=_compiler_params(1),
        name="merge",
    )(x, u, u, u, o_da, qca, kt, v, g_pre, w_gate, b_gate, pool_w, pool_scale,
      w_pool, w_da, w_ca, w_out, g_post)


def _block_diag(w):
    depth, groups, c, _ = w.shape
    eye = jnp.eye(groups, dtype=w.dtype)
    return jnp.einsum("lgcd,gh->lgchd", w, eye).reshape(depth, groups * c, groups * c)


def kernel(x, mem, positions, ffn1_pre_g, ffn1_w_up, ffn1_w_down, ffn1_post_g, mix_pre_g, w_in, pool_w, pool_scale, da_lambda_q1, da_lambda_k1, da_lambda_q2, da_lambda_k2, da_subln_g, mem_norm_g, w_mem_kv, w_gate, b_gate, w_br_pool, w_br_da, w_br_ca, w_out, mix_post_g, ffn2_pre_g, ffn2_w_up, ffn2_w_down, ffn2_post_g):
    batch, seq, d = x.shape
    depth = w_in.shape[0]
    assert seq % TOKEN_TILE == 0 and seq % min(WIDE_TILE, seq) == 0 and seq % ATTN_Q_TILE == 0
    assert w_in.shape[-1] == POOL_WIDTH + 3 * DA_WIDTH + CA_WIDTH

    vec = lambda a: a.astype(F32).reshape(depth, 1, a.shape[-1])
    bf = lambda a: a.astype(BF16)

    cos_t, sin_t = _rope_tables(positions)
    kt_all, v_all = _mem_kv(mem, vec(mem_norm_g), bf(w_mem_kv))

    ffn1 = (vec(ffn1_pre_g), bf(ffn1_w_up), bf(ffn1_w_down), vec(ffn1_post_g))
    ffn2 = (vec(ffn2_pre_g), bf(ffn2_w_up), bf(ffn2_w_down), vec(ffn2_post_g))
    mix_g, w_in_b = vec(mix_pre_g), bf(w_in)
    lam_vecs = tuple(vec(a) for a in (da_lambda_q1, da_lambda_k1, da_lambda_q2, da_lambda_k2))
    merge_params = (mix_g, bf(w_gate), vec(b_gate), bf(_block_diag(pool_w)), vec(pool_scale),
                    bf(w_br_pool), bf(w_br_da), bf(w_br_ca), bf(w_out), vec(mix_post_g))
    subln_g = vec(da_subln_g)

    xt = x.reshape(batch * seq, d)
    for layer in range(depth):
        lam_init = 0.8 - 0.6 * math.exp(-0.3 * layer)
        xt = _ffn(xt, *ffn1, layer)
        u, q8, k8, vt, qca = _inproj(xt, mix_g, w_in_b, cos_t, sin_t, layer, batch, seq)
        o_da = _attention(q8, k8, vt, *lam_vecs, subln_g, layer, lam_init)
        xt = _merge(xt, u, o_da, qca, kt_all, v_all, *merge_params, layer, seq)
        xt = _ffn(xt, *ffn2, layer)
    return xt.reshape(batch, seq, d)
```

```python
import functools
import math

import jax
import jax.numpy as jnp
import numpy as np
from jax import lax
from jax.experimental import pallas as pl
from jax.experimental.pallas import tpu as pltpu

F32 = jnp.float32
BF16 = jnp.bfloat16
F8 = jnp.float8_e4m3fn

POOL_WINDOWS = (2, 4, 8, 16)
POOL_WIDTH = 256
POOL_GROUP_DIM = POOL_WIDTH // len(POOL_WINDOWS)
POOL_HALO = max(POOL_WINDOWS) // 2
DA_HEADS = 4
DA_HEAD_DIM = 64
DA_V_DIM = 2 * DA_HEAD_DIM
DA_WIDTH = DA_HEADS * DA_V_DIM
CA_HEADS = 4
CA_HEAD_DIM = 64
CA_WIDTH = CA_HEADS * CA_HEAD_DIM
ROPE_THETA = 500000.0
ROPE_DIM = DA_HEAD_DIM // 4
ROPE_HALF = ROPE_DIM // 2
NORM_EPS = 1e-6
LOG2E = math.log2(math.e)

V7X_LANES = 128
V7X_SUBLANES = 8
V7X_VMEM_LIMIT_BYTES = 60000 * 1024

TOKEN_TILE = 1024
FFN_PARTS = 2
WIDE_TILE = 1024
ATTN_Q_TILE = 512
ATTN_K_TILE = 2048
ATTN_ROW_CHUNK = 1024
BF16_SUBLANE_TILE = 2 * V7X_SUBLANES
VT_ROWS = DA_V_DIM + BF16_SUBLANE_TILE
V7X_MXU_DEPTH = 256
QK_STACK = V7X_MXU_DEPTH


def _compiler_params(n_grid_axes):
    return pltpu.CompilerParams(
        dimension_semantics=("arbitrary",) * n_grid_axes,
        vmem_limit_bytes=V7X_VMEM_LIMIT_BYTES,
    )


def _resident(block_shape, index_map):
    return pl.BlockSpec(block_shape, index_map, pipeline_mode=pl.Buffered(1))


def _rms_norm(x, g):
    ms = jnp.mean(x * x, axis=-1, keepdims=True)
    return x * lax.rsqrt(ms + NORM_EPS) * g


def _dot(a, b):
    return jnp.dot(a, b, preferred_element_type=F32)


def _dot_t(a_t, b):
    return lax.dot_general(a_t, b, (((0,), (0,)), ((), ())), preferred_element_type=F32)


def _rope_table_kernel(pos_ref, cos_ref, sin_ref):
    pos = pos_ref[...].astype(F32)
    lane = lax.broadcasted_iota(jnp.int32, pos.shape, 1)
    in_comp = lane % DA_HEAD_DIM
    freq_idx = lane % ROPE_HALF
    inv_freq = (np.float32(ROPE_THETA) ** (-np.arange(0, ROPE_DIM, 2, dtype=np.float32) / np.float32(ROPE_DIM)))
    inv = jnp.full(pos.shape, float(inv_freq[0]), F32)
    for j in range(1, ROPE_HALF):
        inv = jnp.where(freq_idx == j, float(inv_freq[j]), inv)
    ang = pos * inv
    rotary = in_comp < ROPE_DIM
    cos_ref[...] = jnp.where(rotary, jnp.cos(ang), 1.0)
    sin_v = jnp.sin(ang)
    sin_ref[...] = jnp.where(rotary, jnp.where(in_comp < ROPE_HALF, -sin_v, sin_v), 0.0)


def _rope_tables(positions):
    n = positions.size
    pos = jnp.broadcast_to(positions.reshape(n, 1), (n, V7X_LANES))
    spec = pl.BlockSpec((TOKEN_TILE, V7X_LANES), lambda i: (i, 0))
    return pl.pallas_call(
        _rope_table_kernel,
        grid=(n // TOKEN_TILE,),
        in_specs=[spec],
        out_specs=[spec, spec],
        out_shape=[jax.ShapeDtypeStruct((n, V7X_LANES), F32)] * 2,
        compiler_params=_compiler_params(1),
        name="rope_tables",
    )(pos)


def _ffn_kernel(x_ref, g_pre_ref, w_up_ref, w_down_ref, g_post_ref, o_ref, *, d_ff):
    rows_per_part = x_ref.shape[0] // FFN_PARTS
    for part in range(FFN_PARTS):
        rows = slice(part * rows_per_part, (part + 1) * rows_per_part)
        x = x_ref[rows, :]
        h = _rms_norm(x, g_pre_ref[...]).astype(BF16)
        ab = _dot(h, w_up_ref[...])
        a = ab[:, :d_ff]
        b = ab[:, d_ff:]
        act = (a * jax.nn.sigmoid(a) * b).astype(BF16)
        y = _dot(act, w_down_ref[...])
        o_ref[rows, :] = x + 0.5 * _rms_norm(y, g_post_ref[...])


def _ffn(x, g_pre, w_up, w_down, g_post, layer):
    n, d = x.shape
    d_ff = w_down.shape[1]
    row = pl.BlockSpec((TOKEN_TILE, d), lambda i: (i, 0))
    gain = _resident((None, 1, d), lambda i: (layer, 0, 0))
    return pl.pallas_call(
        functools.partial(_ffn_kernel, d_ff=d_ff),
        grid=(n // TOKEN_TILE,),
        in_specs=[
            row,
            gain,
            _resident((None, d, 2 * d_ff), lambda i: (layer, 0, 0)),
            _resident((None, d_ff, d), lambda i: (layer, 0, 0)),
            gain,
        ],
        out_specs=row,
        out_shape=jax.ShapeDtypeStruct((n, d), F32),
        compiler_params=_compiler_params(1),
        name="ffn",
    )(x, g_pre, w_up, w_down, g_post)


def _apply_rope(t, cos_t, sin_t):
    width = t.shape[-1]
    lane = lax.broadcasted_iota(jnp.int32, t.shape, 1) % DA_HEAD_DIM
    partner = jnp.where(lane < ROPE_HALF,
                        pltpu.roll(t, width - ROPE_HALF, 1),
                        pltpu.roll(t, ROPE_HALF, 1))
    reps = width // cos_t.shape[-1]
    cos_w = jnp.concatenate([cos_t] * reps, axis=-1)
    sin_w = jnp.concatenate([sin_t] * reps, axis=-1)
    return t * cos_w + partner * sin_w


def _split_fp8(x):
    fp8_max = float(jnp.finfo(F8).max)
    x = jnp.clip(x, -fp8_max, fp8_max)
    hi = x.astype(F8)
    return hi, x - hi.astype(F32)


def _inproj_kernel(x_ref, g_ref, w_ref, cos_ref, sin_ref,
                   u_ref, q8_ref, k8_ref, vt_ref, qca_ref):
    h = _rms_norm(x_ref[...], g_ref[...]).astype(BF16)
    proj = _dot(h, w_ref[...])
    tm = proj.shape[0]
    o_q = POOL_WIDTH
    o_k = o_q + DA_WIDTH
    o_v = o_k + DA_WIDTH
    o_ca = o_v + DA_WIDTH
    cos_t = cos_ref[...]
    sin_t = sin_ref[...]
    u_ref[...] = proj[:, :o_q]
    qk_scale = math.sqrt(DA_HEAD_DIM ** -0.5 * LOG2E)
    q_t = (_apply_rope(proj[:, o_q:o_k], cos_t, sin_t) * qk_scale).T
    k = _apply_rope(proj[:, o_k:o_v], cos_t, sin_t) * qk_scale

    first_half = lax.broadcasted_iota(jnp.int32, (tm, DA_V_DIM), 1) < DA_HEAD_DIM
    zero_rows = jnp.zeros((DA_HEAD_DIM, tm), F8)
    for hd in range(DA_HEADS):
        k_head = k[:, hd * DA_V_DIM:(hd + 1) * DA_V_DIM]
        k_swap = pltpu.roll(k_head, DA_HEAD_DIM, 1)
        for c in range(2):
            both = jnp.where(first_half, k_head, k_swap) if c == 0 else jnp.where(first_half, k_swap, k_head)
            hi, lo = _split_fp8(both)
            col = (2 * hd + c) * QK_STACK
            k8_ref[:, col:col + DA_V_DIM] = hi
            k8_ref[:, col + DA_V_DIM:col + QK_STACK] = jnp.where(first_half, lo, 0.0).astype(F8)
            row = hd * DA_V_DIM + c * DA_HEAD_DIM
            hi, lo = _split_fp8(q_t[row:row + DA_HEAD_DIM])
            base = c * QK_STACK
            q8_ref[hd, base:base + DA_HEAD_DIM] = hi
            q8_ref[hd, base + DA_HEAD_DIM:base + 2 * DA_HEAD_DIM] = lo.astype(F8)
            q8_ref[hd, base + 2 * DA_HEAD_DIM:base + 3 * DA_HEAD_DIM] = hi
            q8_ref[hd, base + 3 * DA_HEAD_DIM:base + QK_STACK] = zero_rows
    vt = proj[:, o_v:o_ca].T.astype(BF16)
    ones_row = (lax.broadcasted_iota(jnp.int32, (BF16_SUBLANE_TILE, vt.shape[1]), 0) == 0).astype(BF16)
    for hd in range(DA_HEADS):
        vt_ref[hd, :DA_V_DIM, :] = vt[hd * DA_V_DIM:(hd + 1) * DA_V_DIM]
        vt_ref[hd, DA_V_DIM:, :] = ones_row
    qca_ref[...] = (proj[:, o_ca:] * (CA_HEAD_DIM ** -0.5 * LOG2E)).astype(BF16)


def _inproj(x, g, w_in, cos_t, sin_t, layer, batch, seq):
    n, d = x.shape
    tm = min(WIDE_TILE, seq)
    tiles_per_seq = seq // tm
    row = lambda width: pl.BlockSpec((tm, width), lambda i: (i, 0))
    col = lambda rows: pl.BlockSpec((None, DA_HEADS, rows, tm),
                                    lambda i: (i // tiles_per_seq, 0, 0, i % tiles_per_seq))
    return pl.pallas_call(
        _inproj_kernel,
        grid=(n // tm,),
        in_specs=[
            row(d),
            _resident((None, 1, d), lambda i: (layer, 0, 0)),
            _resident((None, d, w_in.shape[-1]), lambda i: (layer, 0, 0)),
            row(V7X_LANES),
            row(V7X_LANES),
        ],
        out_specs=[row(POOL_WIDTH), col(2 * QK_STACK), row(DA_HEADS * 2 * QK_STACK), col(VT_ROWS), row(CA_WIDTH)],
        out_shape=[
            jax.ShapeDtypeStruct((n, POOL_WIDTH), F32),
            jax.ShapeDtypeStruct((batch, DA_HEADS, 2 * QK_STACK, seq), F8),
            jax.ShapeDtypeStruct((n, DA_HEADS * 2 * QK_STACK), F8),
            jax.ShapeDtypeStruct((batch, DA_HEADS, VT_ROWS, seq), BF16),
            jax.ShapeDtypeStruct((n, CA_WIDTH), BF16),
        ],
        compiler_params=_compiler_params(1),
        name="inproj",
    )(x, g, w_in, cos_t, sin_t)


def _attn_kernel(q8_ref, k8_ref, vt_ref, lq1_ref, lk1_ref, lq2_ref, lk2_ref, g_ref, o_ref,
                 s_even, s_odd, acc_ref, *, lam_init, seq, tq, tk, chunk):
    n_blk = seq // tk
    n_steps = (seq // tq) * n_blk
    neg_inf_rows = jnp.full((V7X_SUBLANES, tq), -jnp.inf, F32)

    def score_chunk(q_start, k_start, r, s_buf, part_max):
        rows = pl.ds(pl.multiple_of(r, chunk), chunk)
        out = []
        for c in range(2):
            cols = slice(c * QK_STACK, (c + 1) * QK_STACK)
            s = _dot(k8_ref[pl.ds(pl.multiple_of(k_start + r, chunk), chunk), cols],
                     q8_ref[cols, pl.ds(q_start, tq)])
            s_buf[c, rows, :] = s
            s_max = jnp.max(s.reshape(-1, V7X_SUBLANES, tq), axis=0)
            out.append(jnp.maximum(part_max[c], s_max))
        return tuple(out)

    def softmax_chunk(k_start, r, s_buf, m_new, rescale):
        rows = pl.ds(pl.multiple_of(r, chunk), chunk)
        vt_rows = vt_ref[:, pl.ds(pl.multiple_of(k_start + r, chunk), chunk)]
        for c in range(2):
            p = jnp.exp2((s_buf[c, rows, :] - m_new[c]).astype(BF16))
            acc_ref[c] = acc_ref[c] * rescale[c] + _dot(vt_rows, p)

    def block_max(part_max):
        return tuple(jnp.max(pm, axis=0, keepdims=True) for pm in part_max)

    def step(t, t_next, s_cur, s_next, m, cur_max):
        k_blk = t % n_blk
        k_start = k_blk * tk
        next_q_start = pl.multiple_of((t_next // n_blk) * tq, tq)
        next_k_start = (t_next % n_blk) * tk
        m_new, alpha = [], []
        for c in range(2):
            m_prev = jnp.where(k_blk == 0, -jnp.inf, m[c])
            m_new.append(jnp.maximum(m_prev, cur_max[c]))
            alpha.append(jnp.exp2(m_prev - m_new[c]))
        m_new = tuple(m_new)

        def body(ci, part_max):
            r = ci * chunk
            part_max = score_chunk(next_q_start, next_k_start, r, s_next, part_max)
            softmax_chunk(k_start, r, s_cur, m_new, [jnp.where(ci == 0, a, 1.0) for a in alpha])
            return part_max

        part_max = lax.fori_loop(0, tk // chunk, body, (neg_inf_rows, neg_inf_rows))
        return m_new, block_max(part_max)

    lam = (jnp.exp(jnp.sum(lq1_ref[...] * lk1_ref[...], axis=-1, keepdims=True))
           - jnp.exp(jnp.sum(lq2_ref[...] * lk2_ref[...], axis=-1, keepdims=True))
           + lam_init)

    def finalize(q_tile):
        acc0 = acc_ref[0]
        acc1 = acc_ref[1]
        o_t = (acc0[:DA_V_DIM] * (1.0 / acc0[DA_V_DIM:DA_V_DIM + 1])
               - lam * (acc1[:DA_V_DIM] * (1.0 / acc1[DA_V_DIM:DA_V_DIM + 1])))
        ms = jnp.mean(o_t * o_t, axis=0, keepdims=True)
        o = o_t * lax.rsqrt(ms + NORM_EPS) * g_ref[...] * (1.0 - lam_init)
        o_ref[:, pl.ds(pl.multiple_of(q_tile * tq, tq), tq)] = o.astype(BF16)

    acc_ref[...] = jnp.zeros(acc_ref.shape, F32)
    first_max = lax.fori_loop(
        0, tk // chunk,
        lambda ci, pm: score_chunk(0, 0, ci * chunk, s_even, pm),
        (neg_inf_rows, neg_inf_rows))
    neg_inf = jnp.full((1, tq), -jnp.inf, F32)

    def pair(i, carry):
        m, max_even = carry
        t = 2 * i
        m, max_odd = step(t, t + 1, s_even, s_odd, m, max_even)
        m, max_even = step(t + 1, jnp.where(t + 2 == n_steps, 0, t + 2), s_odd, s_even, m, max_odd)

        @pl.when((t + 1) % n_blk == n_blk - 1)
        def _():
            finalize((t + 1) // n_blk)

        return m, max_even

    lax.fori_loop(0, n_steps // 2, pair, ((neg_inf, neg_inf), block_max(first_max)))


def _attention(q8, k8, vt, lq1, lk1, lq2, lk2, g, layer, lam_init):
    batch, _, _, seq = q8.shape
    k8 = k8.reshape(batch, seq, DA_HEADS * 2 * QK_STACK)
    tq = ATTN_Q_TILE
    tk = min(ATTN_K_TILE, seq // 2)
    chunk = min(ATTN_ROW_CHUNK, tk)
    assert seq % tq == 0 and seq % (2 * tk) == 0 and tk % chunk == 0
    small = lambda width: _resident((None, 1, width), lambda b, h: (layer, 0, 0))
    g = g.reshape(g.shape[0], DA_V_DIM, 1)
    return pl.pallas_call(
        functools.partial(_attn_kernel, lam_init=lam_init, seq=seq, tq=tq, tk=tk, chunk=chunk),
        grid=(batch, DA_HEADS),
        in_specs=[
            pl.BlockSpec((None, None, 2 * QK_STACK, seq), lambda b, h: (b, h, 0, 0)),
            pl.BlockSpec((None, seq, 2 * QK_STACK), lambda b, h: (b, 0, h)),
            pl.BlockSpec((None, None, VT_ROWS, seq), lambda b, h: (b, h, 0, 0)),
            small(DA_HEAD_DIM), small(DA_HEAD_DIM), small(DA_HEAD_DIM), small(DA_HEAD_DIM),
            _resident((None, DA_V_DIM, 1), lambda b, h: (layer, 0, 0)),
        ],
        out_specs=pl.BlockSpec((None, DA_V_DIM, seq), lambda b, h: (b, h, 0)),
        out_shape=jax.ShapeDtypeStruct((batch, DA_WIDTH, seq), BF16),
        scratch_shapes=[
            pltpu.VMEM((2, tk, tq), F32),
            pltpu.VMEM((2, tk, tq), F32),
            pltpu.VMEM((2, VT_ROWS, tq), F32),
        ],
        compiler_params=_compiler_params(2),
        name="diff_attention",
    )(q8, k8, vt, lq1, lk1, lq2, lk2, g)


def _mem_kv_kernel(mem_ref, g_ref, w_ref, kt_ref, v_ref):
    hm = _rms_norm(mem_ref[...], g_ref[...]).astype(BF16)
    kv = _dot(hm, w_ref[...])
    k_t = kv[:, :CA_WIDTH].T
    v = kv[:, CA_WIDTH:]
    k_head = lax.broadcasted_iota(jnp.int32, k_t.shape, 0) // CA_HEAD_DIM
    v_head = lax.broadcasted_iota(jnp.int32, v.shape, 1) // CA_HEAD_DIM
    for h in range(CA_HEADS):
        kt_ref[h] = jnp.where(k_head == h, k_t, 0.0).astype(BF16)
        v_ref[h] = jnp.where(v_head == h, v, 0.0).astype(BF16)


def _mem_kv(mem, g, w):
    batch, n_mem, d = mem.shape
    depth = w.shape[0]
    out_spec = lambda r, c: pl.BlockSpec((None, None, CA_HEADS, r, c), lambda l, b: (l, b, 0, 0, 0))
    return pl.pallas_call(
        _mem_kv_kernel,
        grid=(depth, batch),
        in_specs=[
            pl.BlockSpec((None, n_mem, d), lambda l, b: (b, 0, 0)),
            pl.BlockSpec((None, 1, d), lambda l, b: (l, 0, 0)),
            pl.BlockSpec((None, d, 2 * CA_WIDTH), lambda l, b: (l, 0, 0)),
        ],
        out_specs=[out_spec(CA_WIDTH, n_mem), out_spec(n_mem, CA_WIDTH)],
        out_shape=[
            jax.ShapeDtypeStruct((depth, batch, CA_HEADS, CA_WIDTH, n_mem), BF16),
            jax.ShapeDtypeStruct((depth, batch, CA_HEADS, n_mem, CA_WIDTH), BF16),
        ],
        compiler_params=_compiler_params(2),
        name="mem_kv",
    )(mem, g, w)


def _pooled(u_prev_ref, u_ref, u_next_ref, ext_ref, seq):
    tm = u_ref.shape[0]
    tiles_per_seq = seq // tm
    t = pl.program_id(0) % tiles_per_seq
    u = u_ref[...]
    ext_ref[pl.ds(0, POOL_HALO), :] = jnp.where(t > 0, u_prev_ref[...], 0.0)
    ext_ref[pl.ds(POOL_HALO, tm), :] = u
    ext_ref[pl.ds(POOL_HALO + tm, POOL_HALO), :] = jnp.where(t < tiles_per_seq - 1, u_next_ref[...], 0.0)

    def window(lo, hi, lanes):
        acc = ext_ref[pl.ds(POOL_HALO + lo, tm), lanes]
        for d in range(lo + 1, hi):
            acc = acc + ext_ref[pl.ds(POOL_HALO + d, tm), lanes]
        return acc

    groups_per_tile = V7X_LANES // POOL_GROUP_DIM
    shape = (tm, V7X_LANES)
    pos = t * tm + lax.broadcasted_iota(jnp.int32, shape, 0)
    group_in_tile = lax.broadcasted_iota(jnp.int32, shape, 1) // POOL_GROUP_DIM
    tiles = []
    for tile in range(POOL_WIDTH // V7X_LANES):
        lanes = slice(tile * V7X_LANES, (tile + 1) * V7X_LANES)
        total = jnp.zeros(shape, F32)
        count = jnp.ones(shape, F32)
        for gi in range(groups_per_tile):
            half = POOL_WINDOWS[tile * groups_per_tile + gi] // 2
            cnt = (jnp.minimum(pos + half, seq) - jnp.maximum(pos - half, 0)).astype(F32)
            total = jnp.where(group_in_tile == gi, window(-half, half, lanes), total)
            count = jnp.where(group_in_tile == gi, cnt, count)
        tiles.append(total / count)
    return jnp.concatenate(tiles, axis=-1) - u


def _merge_kernel(x_ref, u_prev_ref, u_ref, u_next_ref, o_da_ref, qca_ref, kt_ref, v_ref,
                  g_pre_ref, w_gate_ref, b_gate_ref, pool_w_ref, pool_scale_ref,
                  w_pool_ref, w_da_ref, w_ca_ref, w_out_ref, g_post_ref,
                  out_ref, ext_ref, *, seq):
    x = x_ref[...]
    d = x.shape[-1]
    h = _rms_norm(x, g_pre_ref[...]).astype(BF16)
    gates = jax.nn.sigmoid(_dot(h, w_gate_ref[...]) + b_gate_ref[...])

    pooled = _pooled(u_prev_ref, u_ref, u_next_ref, ext_ref, seq).astype(BF16)
    y_pool = (_dot(pooled, pool_w_ref[...]) * pool_scale_ref[...]).astype(BF16)

    qca = qca_ref[...]
    y_ca = jnp.zeros(qca.shape, F32)
    for hd in range(CA_HEADS):
        s = _dot(qca, kt_ref[hd])
        p = jnp.exp2(s - jnp.max(s, axis=-1, keepdims=True))
        p = p * (1.0 / jnp.sum(p, axis=-1, keepdims=True))
        y_ca = y_ca + _dot(p.astype(BF16), v_ref[hd])

    merged = (gates[:, :d] * _dot(y_pool, w_pool_ref[...])
              + gates[:, d:2 * d] * _dot_t(o_da_ref[...], w_da_ref[...])
              + gates[:, 2 * d:] * _dot(y_ca.astype(BF16), w_ca_ref[...]))
    y = _dot(merged.astype(BF16), w_out_ref[...])
    out_ref[...] = x + _rms_norm(y, g_post_ref[...])


def _merge(x, u, o_da, qca, kt, v, g_pre, w_gate, b_gate, pool_w, pool_scale,
           w_pool, w_da, w_ca, w_out, g_post, layer, seq):
    n, d = x.shape
    n_mem = v.shape[-2]
    tm = min(WIDE_TILE, seq)
    tiles_per_seq = seq // tm
    halo_blocks = tm // POOL_HALO
    n_halo = n // POOL_HALO
    row = lambda width: pl.BlockSpec((tm, width), lambda i: (i, 0))
    layer_block = lambda *shape: _resident((None,) + shape, lambda i: (layer,) + (0,) * len(shape))
    mem_block = lambda r, c: pl.BlockSpec((None, None, CA_HEADS, r, c),
                                          lambda i: (layer, i // tiles_per_seq, 0, 0, 0))
    return pl.pallas_call(
        functools.partial(_merge_kernel, seq=seq),
        grid=(n // tm,),
        in_specs=[
            row(d),
            pl.BlockSpec((POOL_HALO, POOL_WIDTH), lambda i: (jnp.maximum(i * halo_blocks - 1, 0), 0)),
            row(POOL_WIDTH),
            pl.BlockSpec((POOL_HALO, POOL_WIDTH), lambda i: (jnp.minimum((i + 1) * halo_blocks, n_halo - 1), 0)),
            pl.BlockSpec((None, DA_WIDTH, tm), lambda i: (i // tiles_per_seq, 0, i % tiles_per_seq)),
            row(CA_WIDTH),
            mem_block(CA_WIDTH, n_mem),
            mem_block(n_mem, CA_WIDTH),
            layer_block(1, d),
            layer_block(d, 3 * d),
            layer_block(1, 3 * d),
            layer_block(POOL_WIDTH, POOL_WIDTH),
            layer_block(1, POOL_WIDTH),
            layer_block(POOL_WIDTH, d),
            layer_block(DA_WIDTH, d),
            layer_block(CA_WIDTH, d),
            layer_block(d, d),
            layer_block(1, d),
        ],
        out_specs=row(d),
        out_shape=jax.ShapeDtypeStruct((n, d), F32),
        scratch_shapes=[pltpu.VMEM((tm + 2 * POOL_HALO, POOL_WIDTH), F32)],
        compiler_params=_compiler_params(1),
        name="merge",
    )(x, u, u, u, o_da, qca, kt, v, g_pre, w_gate, b_gate, pool_w, pool_scale,
      w_pool, w_da, w_ca, w_out, g_post)


def _block_diag(w):
    depth, groups, c, _ = w.shape
    eye = jnp.eye(groups, dtype=w.dtype)
    return jnp.einsum("lgcd,gh->lgchd", w, eye).reshape(depth, groups * c, groups * c)


def kernel(x, mem, positions, ffn1_pre_g, ffn1_w_up, ffn1_w_down, ffn1_post_g, mix_pre_g, w_in, pool_w, pool_scale, da_lambda_q1, da_lambda_k1, da_lambda_q2, da_lambda_k2, da_subln_g, mem_norm_g, w_mem_kv, w_gate, b_gate, w_br_pool, w_br_da, w_br_ca, w_out, mix_post_g, ffn2_pre_g, ffn2_w_up, ffn2_w_down, ffn2_post_g):
    batch, seq, d = x.shape
    depth = w_in.shape[0]
    assert seq % TOKEN_TILE == 0 and seq % min(WIDE_TILE, seq) == 0 and seq % ATTN_Q_TILE == 0
    assert w_in.shape[-1] == POOL_WIDTH + 3 * DA_WIDTH + CA_WIDTH

    vec = lambda a: a.astype(F32).reshape(depth, 1, a.shape[-1])
    bf = lambda a: a.astype(BF16)

    cos_t, sin_t = _rope_tables(positions)
    kt_all, v_all = _mem_kv(mem, vec(mem_norm_g), bf(w_mem_kv))

    ffn1 = (vec(ffn1_pre_g), bf(ffn1_w_up), bf(ffn1_w_down), vec(ffn1_post_g))
    ffn2 = (vec(ffn2_pre_g), bf(ffn2_w_up), bf(ffn2_w_down), vec(ffn2_post_g))
    mix_g, w_in_b = vec(mix_pre_g), bf(w_in)
    lam_vecs = tuple(vec(a) for a in (da_lambda_q1, da_lambda_k1, da_lambda_q2, da_lambda_k2))
    merge_params = (mix_g, bf(w_gate), vec(b_gate), bf(_block_diag(pool_w)), vec(pool_scale),
                    bf(w_br_pool), bf(w_br_da), bf(w_br_ca), bf(w_out), vec(mix_post_g))
    subln_g = vec(da_subln_g)

    xt = x.reshape(batch * seq, d)
    for layer in range(depth):
        lam_init = 0.8 - 0.6 * math.exp(-0.3 * layer)
        xt = _ffn(xt, *ffn1, layer)
        u, q8, k8, vt, qca = _inproj(xt, mix_g, w_in_b, cos_t, sin_t, layer, batch, seq)
        o_da = _attention(q8, k8, vt, *lam_vecs, subln_g, layer, lam_init)
        xt = _merge(xt, u, o_da, qca, kt_all, v_all, *merge_params, layer, seq)
        xt = _ffn(xt, *ffn2, layer)
    return xt.reshape(batch, seq, d)
```

```python
import functools
import math

import jax
import jax.numpy as jnp
import numpy as np
from jax import lax
from jax.experimental import pallas as pl
from jax.experimental.pallas import tpu as pltpu

F32 = jnp.float32
BF16 = jnp.bfloat16
F8 = jnp.float8_e4m3fn

POOL_WINDOWS = (2, 4, 8, 16)
POOL_WIDTH = 256
POOL_GROUP_DIM = POOL_WIDTH // len(POOL_WINDOWS)
POOL_HALO = max(POOL_WINDOWS) // 2
DA_HEADS = 4
DA_HEAD_DIM = 64
DA_V_DIM = 2 * DA_HEAD_DIM
DA_WIDTH = DA_HEADS * DA_V_DIM
CA_HEADS = 4
CA_HEAD_DIM = 64
CA_WIDTH = CA_HEADS * CA_HEAD_DIM
ROPE_THETA = 500000.0
ROPE_DIM = DA_HEAD_DIM // 4
ROPE_HALF = ROPE_DIM // 2
NORM_EPS = 1e-6
LOG2E = math.log2(math.e)

V7X_LANES = 128
V7X_SUBLANES = 8
V7X_VMEM_LIMIT_BYTES = 60000 * 1024

TOKEN_TILE = 1024
FFN_PARTS = 4
INPROJ_PARTS = 2
WIDE_TILE = 1024
ATTN_Q_TILE = 512
ATTN_K_TILE = 2048
ATTN_ROW_CHUNK = 1024
BF16_SUBLANE_TILE = 2 * V7X_SUBLANES
VT_ROWS = DA_V_DIM + BF16_SUBLANE_TILE
V7X_MXU_DEPTH = 256
QK_STACK = V7X_MXU_DEPTH


def _compiler_params(n_grid_axes):
    return pltpu.CompilerParams(
        dimension_semantics=("arbitrary",) * n_grid_axes,
        vmem_limit_bytes=V7X_VMEM_LIMIT_BYTES,
    )


def _resident(block_shape, index_map):
    return pl.BlockSpec(block_shape, index_map, pipeline_mode=pl.Buffered(1))


def _rms_norm(x, g):
    ms = jnp.mean(x * x, axis=-1, keepdims=True)
    return x * lax.rsqrt(ms + NORM_EPS) * g


def _dot(a, b):
    return jnp.dot(a, b, preferred_element_type=F32)


def _dot_t(a_t, b):
    return lax.dot_general(a_t, b, (((0,), (0,)), ((), ())), preferred_element_type=F32)


def _rope_table_kernel(pos_ref, cos_ref, sin_ref):
    pos = pos_ref[...].astype(F32)
    lane = lax.broadcasted_iota(jnp.int32, pos.shape, 1)
    in_comp = lane % DA_HEAD_DIM
    freq_idx = lane % ROPE_HALF
    inv_freq = (np.float32(ROPE_THETA) ** (-np.arange(0, ROPE_DIM, 2, dtype=np.float32) / np.float32(ROPE_DIM)))
    inv = jnp.full(pos.shape, float(inv_freq[0]), F32)
    for j in range(1, ROPE_HALF):
        inv = jnp.where(freq_idx == j, float(inv_freq[j]), inv)
    ang = pos * inv
    rotary = in_comp < ROPE_DIM
    cos_ref[...] = jnp.where(rotary, jnp.cos(ang), 1.0)
    sin_v = jnp.sin(ang)
    sin_ref[...] = jnp.where(rotary, jnp.where(in_comp < ROPE_HALF, -sin_v, sin_v), 0.0)


def _rope_tables(positions):
    n = positions.size
    pos = jnp.broadcast_to(positions.reshape(n, 1), (n, V7X_LANES))
    spec = pl.BlockSpec((TOKEN_TILE, V7X_LANES), lambda i: (i, 0))
    return pl.pallas_call(
        _rope_table_kernel,
        grid=(n // TOKEN_TILE,),
        in_specs=[spec],
        out_specs=[spec, spec],
        out_shape=[jax.ShapeDtypeStruct((n, V7X_LANES), F32)] * 2,
        compiler_params=_compiler_params(1),
        name="rope_tables",
    )(pos)


def _ffn_kernel(x_ref, g_pre_ref, w_up_ref, w_down_ref, g_post_ref, o_ref, *, d_ff):
    rows_per_part = x_ref.shape[0] // FFN_PARTS
    for part in range(FFN_PARTS):
        rows = slice(part * rows_per_part, (part + 1) * rows_per_part)
        x = x_ref[rows, :]
        h = _rms_norm(x, g_pre_ref[...]).astype(BF16)
        ab = _dot(h, w_up_ref[...])
        a = ab[:, :d_ff]
        b = ab[:, d_ff:]
        act = (a * jax.nn.sigmoid(a) * b).astype(BF16)
        y = _dot(act, w_down_ref[...])
        o_ref[rows, :] = x + 0.5 * _rms_norm(y, g_post_ref[...])


def _ffn(x, g_pre, w_up, w_down, g_post, layer):
    n, d = x.shape
    d_ff = w_down.shape[1]
    row = pl.BlockSpec((TOKEN_TILE, d), lambda i: (i, 0))
    gain = _resident((None, 1, d), lambda i: (layer, 0, 0))
    return pl.pallas_call(
        functools.partial(_ffn_kernel, d_ff=d_ff),
        grid=(n // TOKEN_TILE,),
        in_specs=[
            row,
            gain,
            _resident((None, d, 2 * d_ff), lambda i: (layer, 0, 0)),
            _resident((None, d_ff, d), lambda i: (layer, 0, 0)),
            gain,
        ],
        out_specs=row,
        out_shape=jax.ShapeDtypeStruct((n, d), F32),
        compiler_params=_compiler_params(1),
        name="ffn",
    )(x, g_pre, w_up, w_down, g_post)


def _apply_rope(t, cos_t, sin_t):
    width = t.shape[-1]
    lane = lax.broadcasted_iota(jnp.int32, t.shape, 1) % DA_HEAD_DIM
    partner = jnp.where(lane < ROPE_HALF,
                        pltpu.roll(t, width - ROPE_HALF, 1),
                        pltpu.roll(t, ROPE_HALF, 1))
    reps = width // cos_t.shape[-1]
    cos_w = jnp.concatenate([cos_t] * reps, axis=-1)
    sin_w = jnp.concatenate([sin_t] * reps, axis=-1)
    return t * cos_w + partner * sin_w


def _split_fp8(x):
    fp8_max = float(jnp.finfo(F8).max)
    x = jnp.clip(x, -fp8_max, fp8_max)
    hi = x.astype(F8)
    return hi, x - hi.astype(F32)


def _inproj_kernel(x_ref, g_ref, w_ref, cos_ref, sin_ref,
                   u_ref, q8_ref, k8_ref, vt_ref, qca_ref):
    rows_per_part = x_ref.shape[0] // INPROJ_PARTS
    for part in range(INPROJ_PARTS):
        _inproj_rows(x_ref, g_ref, w_ref, cos_ref, sin_ref, u_ref, q8_ref, k8_ref, vt_ref, qca_ref,
                     slice(part * rows_per_part, (part + 1) * rows_per_part))


def _inproj_rows(x_ref, g_ref, w_ref, cos_ref, sin_ref, u_ref, q8_ref, k8_ref, vt_ref, qca_ref, rows):
    h = _rms_norm(x_ref[rows, :], g_ref[...]).astype(BF16)
    proj = _dot(h, w_ref[...])
    tm = proj.shape[0]
    o_q = POOL_WIDTH
    o_k = o_q + DA_WIDTH
    o_v = o_k + DA_WIDTH
    o_ca = o_v + DA_WIDTH
    cos_t = cos_ref[rows, :]
    sin_t = sin_ref[rows, :]
    u_ref[rows, :] = proj[:, :o_q]
    qk_scale = math.sqrt(DA_HEAD_DIM ** -0.5 * LOG2E)
    q_t = (_apply_rope(proj[:, o_q:o_k], cos_t, sin_t) * qk_scale).T
    k = _apply_rope(proj[:, o_k:o_v], cos_t, sin_t) * qk_scale

    first_half = lax.broadcasted_iota(jnp.int32, (tm, DA_V_DIM), 1) < DA_HEAD_DIM
    zero_rows = jnp.zeros((DA_HEAD_DIM, tm), F8)
    for hd in range(DA_HEADS):
        k_head = k[:, hd * DA_V_DIM:(hd + 1) * DA_V_DIM]
        k_swap = pltpu.roll(k_head, DA_HEAD_DIM, 1)
        for c in range(2):
            both = jnp.where(first_half, k_head, k_swap) if c == 0 else jnp.where(first_half, k_swap, k_head)
            hi, lo = _split_fp8(both)
            col = (2 * hd + c) * QK_STACK
            k8_ref[rows, col:col + DA_V_DIM] = hi
            k8_ref[rows, col + DA_V_DIM:col + QK_STACK] = jnp.where(first_half, lo, 0.0).astype(F8)
            row = hd * DA_V_DIM + c * DA_HEAD_DIM
            hi, lo = _split_fp8(q_t[row:row + DA_HEAD_DIM])
            base = c * QK_STACK
            q8_ref[hd, base:base + DA_HEAD_DIM, rows] = hi
            q8_ref[hd, base + DA_HEAD_DIM:base + 2 * DA_HEAD_DIM, rows] = lo.astype(F8)
            q8_ref[hd, base + 2 * DA_HEAD_DIM:base + 3 * DA_HEAD_DIM, rows] = hi
            q8_ref[hd, base + 3 * DA_HEAD_DIM:base + QK_STACK, rows] = zero_rows
    vt = proj[:, o_v:o_ca].T.astype(BF16)
    ones_row = (lax.broadcasted_iota(jnp.int32, (BF16_SUBLANE_TILE, vt.shape[1]), 0) == 0).astype(BF16)
    for hd in range(DA_HEADS):
        vt_ref[hd, :DA_V_DIM, rows] = vt[hd * DA_V_DIM:(hd + 1) * DA_V_DIM]
        vt_ref[hd, DA_V_DIM:, rows] = ones_row
    qca_ref[rows, :] = (proj[:, o_ca:] * (CA_HEAD_DIM ** -0.5 * LOG2E)).astype(BF16)


def _inproj(x, g, w_in, cos_t, sin_t, layer, batch, seq):
    n, d = x.shape
    tm = min(WIDE_TILE, seq)
    tiles_per_seq = seq // tm
    row = lambda width: pl.BlockSpec((tm, width), lambda i: (i, 0))
    col = lambda rows: pl.BlockSpec((None, DA_HEADS, rows, tm),
                                    lambda i: (i // tiles_per_seq, 0, 0, i % tiles_per_seq))
    return pl.pallas_call(
        _inproj_kernel,
        grid=(n // tm,),
        in_specs=[
            row(d),
            _resident((None, 1, d), lambda i: (layer, 0, 0)),
            _resident((None, d, w_in.shape[-1]), lambda i: (layer, 0, 0)),
            row(V7X_LANES),
            row(V7X_LANES),
        ],
        out_specs=[row(POOL_WIDTH), col(2 * QK_STACK), row(DA_HEADS * 2 * QK_STACK), col(VT_ROWS), row(CA_WIDTH)],
        out_shape=[
            jax.ShapeDtypeStruct((n, POOL_WIDTH), F32),
            jax.ShapeDtypeStruct((batch, DA_HEADS, 2 * QK_STACK, seq), F8),
            jax.ShapeDtypeStruct((n, DA_HEADS * 2 * QK_STACK), F8),
            jax.ShapeDtypeStruct((batch, DA_HEADS, VT_ROWS, seq), BF16),
            jax.ShapeDtypeStruct((n, CA_WIDTH), BF16),
        ],
        compiler_params=_compiler_params(1),
        name="inproj",
    )(x, g, w_in, cos_t, sin_t)


def _attn_kernel(q8_ref, k8_ref, vt_ref, lq1_ref, lk1_ref, lq2_ref, lk2_ref, g_ref, o_ref,
                 s_even, s_odd, acc_ref, *, lam_init, seq, tq, tk, chunk):
    n_blk = seq // tk
    n_steps = (seq // tq) * n_blk
    neg_inf_rows = jnp.full((V7X_SUBLANES, tq), -jnp.inf, F32)

    def score_chunk(q_start, k_start, r, s_buf, part_max):
        rows = pl.ds(pl.multiple_of(r, chunk), chunk)
        out = []
        for c in range(2):
            cols = slice(c * QK_STACK, (c + 1) * QK_STACK)
            s = _dot(k8_ref[pl.ds(pl.multiple_of(k_start + r, chunk), chunk), cols],
                     q8_ref[cols, pl.ds(q_start, tq)])
            s_buf[c, rows, :] = s
            s_max = jnp.max(s.reshape(-1, V7X_SUBLANES, tq), axis=0)
            out.append(jnp.maximum(part_max[c], s_max))
        return tuple(out)

    def softmax_chunk(k_start, r, s_buf, m_new, rescale):
        rows = pl.ds(pl.multiple_of(r, chunk), chunk)
        vt_rows = vt_ref[:, pl.ds(pl.multiple_of(k_start + r, chunk), chunk)]
        for c in range(2):
            p = jnp.exp2((s_buf[c, rows, :] - m_new[c]).astype(BF16))
            acc_ref[c] = acc_ref[c] * rescale[c] + _dot(vt_rows, p)

    def block_max(part_max):
        return tuple(jnp.max(pm, axis=0, keepdims=True) for pm in part_max)

    def step(t, t_next, s_cur, s_next, m, cur_max):
        k_blk = t % n_blk
        k_start = k_blk * tk
        next_q_start = pl.multiple_of((t_next // n_blk) * tq, tq)
        next_k_start = (t_next % n_blk) * tk
        m_new, alpha = [], []
        for c in range(2):
            m_prev = jnp.where(k_blk == 0, -jnp.inf, m[c])
            m_new.append(jnp.maximum(m_prev, cur_max[c]))
            alpha.append(jnp.exp2(m_prev - m_new[c]))
        m_new = tuple(m_new)

        def body(ci, part_max):
            r = ci * chunk
            part_max = score_chunk(next_q_start, next_k_start, r, s_next, part_max)
            softmax_chunk(k_start, r, s_cur, m_new, [jnp.where(ci == 0, a, 1.0) for a in alpha])
            return part_max

        part_max = lax.fori_loop(0, tk // chunk, body, (neg_inf_rows, neg_inf_rows))
        return m_new, block_max(part_max)

    lam = (jnp.exp(jnp.sum(lq1_ref[...] * lk1_ref[...], axis=-1, keepdims=True))
           - jnp.exp(jnp.sum(lq2_ref[...] * lk2_ref[...], axis=-1, keepdims=True))
           + lam_init)

    def finalize(q_tile):
        acc0 = acc_ref[0]
        acc1 = acc_ref[1]
        o_t = (acc0[:DA_V_DIM] * (1.0 / acc0[DA_V_DIM:DA_V_DIM + 1])
               - lam * (acc1[:DA_V_DIM] * (1.0 / acc1[DA_V_DIM:DA_V_DIM + 1])))
        ms = jnp.mean(o_t * o_t, axis=0, keepdims=True)
        o = o_t * lax.rsqrt(ms + NORM_EPS) * g_ref[...] * (1.0 - lam_init)
        o_ref[:, pl.ds(pl.multiple_of(q_tile * tq, tq), tq)] = o.astype(BF16)

    acc_ref[...] = jnp.zeros(acc_ref.shape, F32)
    first_max = lax.fori_loop(
        0, tk // chunk,
        lambda ci, pm: score_chunk(0, 0, ci * chunk, s_even, pm),
        (neg_inf_rows, neg_inf_rows))
    neg_inf = jnp.full((1, tq), -jnp.inf, F32)

    def pair(i, carry):
        m, max_even = carry
        t = 2 * i
        m, max_odd = step(t, t + 1, s_even, s_odd, m, max_even)
        m, max_even = step(t + 1, jnp.where(t + 2 == n_steps, 0, t + 2), s_odd, s_even, m, max_odd)

        @pl.when((t + 1) % n_blk == n_blk - 1)
        def _():
            finalize((t + 1) // n_blk)

        return m, max_even

    lax.fori_loop(0, n_steps // 2, pair, ((neg_inf, neg_inf), block_max(first_max)))


def _attention(q8, k8, vt, lq1, lk1, lq2, lk2, g, layer, lam_init):
    batch, _, _, seq = q8.shape
    k8 = k8.reshape(batch, seq, DA_HEADS * 2 * QK_STACK)
    tq = ATTN_Q_TILE
    tk = min(ATTN_K_TILE, seq // 2)
    chunk = min(ATTN_ROW_CHUNK, tk)
    assert seq % tq == 0 and seq % (2 * tk) == 0 and tk % chunk == 0
    small = lambda width: _resident((None, 1, width), lambda b, h: (layer, 0, 0))
    g = g.reshape(g.shape[0], DA_V_DIM, 1)
    return pl.pallas_call(
        functools.partial(_attn_kernel, lam_init=lam_init, seq=seq, tq=tq, tk=tk, chunk=chunk),
        grid=(batch, DA_HEADS),
        in_specs=[
            pl.BlockSpec((None, None, 2 * QK_STACK, seq), lambda b, h: (b, h, 0, 0)),
            pl.BlockSpec((None, seq, 2 * QK_STACK), lambda b, h: (b, 0, h)),
            pl.BlockSpec((None, None, VT_ROWS, seq), lambda b, h: (b, h, 0, 0)),
            small(DA_HEAD_DIM), small(DA_HEAD_DIM), small(DA_HEAD_DIM), small(DA_HEAD_DIM),
            _resident((None, DA_V_DIM, 1), lambda b, h: (layer, 0, 0)),
        ],
        out_specs=pl.BlockSpec((None, DA_V_DIM, seq), lambda b, h: (b, h, 0)),
        out_shape=jax.ShapeDtypeStruct((batch, DA_WIDTH, seq), BF16),
        scratch_shapes=[
            pltpu.VMEM((2, tk, tq), F32),
            pltpu.VMEM((2, tk, tq), F32),
            pltpu.VMEM((2, VT_ROWS, tq), F32),
        ],
        compiler_params=_compiler_params(2),
        name="diff_attention",
    )(q8, k8, vt, lq1, lk1, lq2, lk2, g)


def _mem_kv_kernel(mem_ref, g_ref, w_ref, kt_ref, v_ref):
    hm = _rms_norm(mem_ref[...], g_ref[...]).astype(BF16)
    kv = _dot(hm, w_ref[...])
    k_t = kv[:, :CA_WIDTH].T
    v = kv[:, CA_WIDTH:]
    k_head = lax.broadcasted_iota(jnp.int32, k_t.shape, 0) // CA_HEAD_DIM
    v_head = lax.broadcasted_iota(jnp.int32, v.shape, 1) // CA_HEAD_DIM
    for h in range(CA_HEADS):
        kt_ref[h] = jnp.where(k_head == h, k_t, 0.0).astype(BF16)
        v_ref[h] = jnp.where(v_head == h, v, 0.0).astype(BF16)


def _mem_kv(mem, g, w):
    batch, n_mem, d = mem.shape
    depth = w.shape[0]
    out_spec = lambda r, c: pl.BlockSpec((None, None, CA_HEADS, r, c), lambda l, b: (l, b, 0, 0, 0))
    return pl.pallas_call(
        _mem_kv_kernel,
        grid=(depth, batch),
        in_specs=[
            pl.BlockSpec((None, n_mem, d), lambda l, b: (b, 0, 0)),
            pl.BlockSpec((None, 1, d), lambda l, b: (l, 0, 0)),
            pl.BlockSpec((None, d, 2 * CA_WIDTH), lambda l, b: (l, 0, 0)),
        ],
        out_specs=[out_spec(CA_WIDTH, n_mem), out_spec(n_mem, CA_WIDTH)],
        out_shape=[
            jax.ShapeDtypeStruct((depth, batch, CA_HEADS, CA_WIDTH, n_mem), BF16),
            jax.ShapeDtypeStruct((depth, batch, CA_HEADS, n_mem, CA_WIDTH), BF16),
        ],
        compiler_params=_compiler_params(2),
        name="mem_kv",
    )(mem, g, w)


def _pooled(u_prev_ref, u_ref, u_next_ref, ext_ref, seq):
    tm = u_ref.shape[0]
    tiles_per_seq = seq // tm
    t = pl.program_id(0) % tiles_per_seq
    u = u_ref[...]
    ext_ref[pl.ds(0, POOL_HALO), :] = jnp.where(t > 0, u_prev_ref[...], 0.0)
    ext_ref[pl.ds(POOL_HALO, tm), :] = u
    ext_ref[pl.ds(POOL_HALO + tm, POOL_HALO), :] = jnp.where(t < tiles_per_seq - 1, u_next_ref[...], 0.0)

    def window(lo, hi, lanes):
        acc = ext_ref[pl.ds(POOL_HALO + lo, tm), lanes]
        for d in range(lo + 1, hi):
            acc = acc + ext_ref[pl.ds(POOL_HALO + d, tm), lanes]
        return acc

    groups_per_tile = V7X_LANES // POOL_GROUP_DIM
    shape = (tm, V7X_LANES)
    pos = t * tm + lax.broadcasted_iota(jnp.int32, shape, 0)
    group_in_tile = lax.broadcasted_iota(jnp.int32, shape, 1) // POOL_GROUP_DIM
    tiles = []
    for tile in range(POOL_WIDTH // V7X_LANES):
        lanes = slice(tile * V7X_LANES, (tile + 1) * V7X_LANES)
        total = jnp.zeros(shape, F32)
        count = jnp.ones(shape, F32)
        for gi in range(groups_per_tile):
            half = POOL_WINDOWS[tile * groups_per_tile + gi] // 2
            cnt = (jnp.minimum(pos + half, seq) - jnp.maximum(pos - half, 0)).astype(F32)
            total = jnp.where(group_in_tile == gi, window(-half, half, lanes), total)
            count = jnp.where(group_in_tile == gi, cnt, count)
        tiles.append(total / count)
    return jnp.concatenate(tiles, axis=-1) - u


def _merge_kernel(x_ref, u_prev_ref, u_ref, u_next_ref, o_da_ref, qca_ref, kt_ref, v_ref,
                  g_pre_ref, w_gate_ref, b_gate_ref, pool_w_ref, pool_scale_ref,
                  w_pool_ref, w_da_ref, w_ca_ref, w_out_ref, g_post_ref,
                  out_ref, ext_ref, *, seq):
    x = x_ref[...]
    d = x.shape[-1]
    h = _rms_norm(x, g_pre_ref[...]).astype(BF16)
    gates = jax.nn.sigmoid(_dot(h, w_gate_ref[...]) + b_gate_ref[...])

    pooled = _pooled(u_prev_ref, u_ref, u_next_ref, ext_ref, seq).astype(BF16)
    y_pool = (_dot(pooled, pool_w_ref[...]) * pool_scale_ref[...]).astype(BF16)

    qca = qca_ref[...]
    y_ca = jnp.zeros(qca.shape, F32)
    for hd in range(CA_HEADS):
        s = _dot(qca, kt_ref[hd])
        p = jnp.exp2(s - jnp.max(s, axis=-1, keepdims=True))
        p = p * (1.0 / jnp.sum(p, axis=-1, keepdims=True))
        y_ca = y_ca + _dot(p.astype(BF16), v_ref[hd])

    merged = (gates[:, :d] * _dot(y_pool, w_pool_ref[...])
              + gates[:, d:2 * d] * _dot_t(o_da_ref[...], w_da_ref[...])
              + gates[:, 2 * d:] * _dot(y_ca.astype(BF16), w_ca_ref[...]))
    y = _dot(merged.astype(BF16), w_out_ref[...])
    out_ref[...] = x + _rms_norm(y, g_post_ref[...])


def _merge(x, u, o_da, qca, kt, v, g_pre, w_gate, b_gate, pool_w, pool_scale,
           w_pool, w_da, w_ca, w_out, g_post, layer, seq):
    n, d = x.shape
    n_mem = v.shape[-2]
    tm = min(WIDE_TILE, seq)
    tiles_per_seq = seq // tm
    halo_blocks = tm // POOL_HALO
    n_halo = n // POOL_HALO
    row = lambda width: pl.BlockSpec((tm, width), lambda i: (i, 0))
    layer_block = lambda *shape: _resident((None,) + shape, lambda i: (layer,) + (0,) * len(shape))
    mem_block = lambda r, c: pl.BlockSpec((None, None, CA_HEADS, r, c),
                                          lambda i: (layer, i // tiles_per_seq, 0, 0, 0))
    return pl.pallas_call(
        functools.partial(_merge_kernel, seq=seq),
        grid=(n // tm,),
        in_specs=[
            row(d),
            pl.BlockSpec((POOL_HALO, POOL_WIDTH), lambda i: (jnp.maximum(i * halo_blocks - 1, 0), 0)),
            row(POOL_WIDTH),
            pl.BlockSpec((POOL_HALO, POOL_WIDTH), lambda i: (jnp.minimum((i + 1) * halo_blocks, n_halo - 1), 0)),
            pl.BlockSpec((None, DA_WIDTH, tm), lambda i: (i // tiles_per_seq, 0, i % tiles_per_seq)),
            row(CA_WIDTH),
            mem_block(CA_WIDTH, n_mem),
            mem_block(n_mem, CA_WIDTH),
            layer_block(1, d),
            layer_block(d, 3 * d),
            layer_block(1, 3 * d),
            layer_block(POOL_WIDTH, POOL_WIDTH),
            layer_block(1, POOL_WIDTH),
            layer_block(POOL_WIDTH, d),
            layer_block(DA_WIDTH, d),
            layer_block(CA_WIDTH, d),
            layer_block(d, d),
            layer_block(1, d),
        ],
        out_specs=row(d),
        out_shape=jax.ShapeDtypeStruct((n, d), F32),
        scratch_shapes=[pltpu.VMEM((tm + 2 * POOL_HALO, POOL_WIDTH), F32)],
        compiler_params=_compiler_params(1),
        name="merge",
    )(x, u, u, u, o_da, qca, kt, v, g_pre, w_gate, b_gate, pool_w, pool_scale,
      w_pool, w_da, w_ca, w_out, g_post)


def _block_diag(w):
    depth, groups, c, _ = w.shape
    eye = jnp.eye(groups, dtype=w.dtype)
    return jnp.einsum("lgcd,gh->lgchd", w, eye).reshape(depth, groups * c, groups * c)


def kernel(x, mem, positions, ffn1_pre_g, ffn1_w_up, ffn1_w_down, ffn1_post_g, mix_pre_g, w_in, pool_w, pool_scale, da_lambda_q1, da_lambda_k1, da_lambda_q2, da_lambda_k2, da_subln_g, mem_norm_g, w_mem_kv, w_gate, b_gate, w_br_pool, w_br_da, w_br_ca, w_out, mix_post_g, ffn2_pre_g, ffn2_w_up, ffn2_w_down, ffn2_post_g):
    batch, seq, d = x.shape
    depth = w_in.shape[0]
    assert seq % TOKEN_TILE == 0 and seq % min(WIDE_TILE, seq) == 0 and seq % ATTN_Q_TILE == 0
    assert w_in.shape[-1] == POOL_WIDTH + 3 * DA_WIDTH + CA_WIDTH

    vec = lambda a: a.astype(F32).reshape(depth, 1, a.shape[-1])
    bf = lambda a: a.astype(BF16)

    cos_t, sin_t = _rope_tables(positions)
    kt_all, v_all = _mem_kv(mem, vec(mem_norm_g), bf(w_mem_kv))

    ffn1 = (vec(ffn1_pre_g), bf(ffn1_w_up), bf(ffn1_w_down), vec(ffn1_post_g))
    ffn2 = (vec(ffn2_pre_g), bf(ffn2_w_up), bf(ffn2_w_down), vec(ffn2_post_g))
    mix_g, w_in_b = vec(mix_pre_g), bf(w_in)
    lam_vecs = tuple(vec(a) for a in (da_lambda_q1, da_lambda_k1, da_lambda_q2, da_lambda_k2))
    merge_params = (mix_g, bf(w_gate), vec(b_gate), bf(_block_diag(pool_w)), vec(pool_scale),
                    bf(w_br_pool), bf(w_br_da), bf(w_br_ca), bf(w_out), vec(mix_post_g))
    subln_g = vec(da_subln_g)

    xt = x.reshape(batch * seq, d)
    for layer in range(depth):
        lam_init = 0.8 - 0.6 * math.exp(-0.3 * layer)
        xt = _ffn(xt, *ffn1, layer)
        u, q8, k8, vt, qca = _inproj(xt, mix_g, w_in_b, cos_t, sin_t, layer, batch, seq)
        o_da = _attention(q8, k8, vt, *lam_vecs, subln_g, layer, lam_init)
        xt = _merge(xt, u, o_da, qca, kt_all, v_all, *merge_params, layer, seq)
        xt = _ffn(xt, *ffn2, layer)
    return xt.reshape(batch, seq, d)
```

```python
import functools
import math

import jax
import jax.numpy as jnp
import numpy as np
from jax import lax
from jax.experimental import pallas as pl
from jax.experimental.pallas import tpu as pltpu

F32 = jnp.float32
BF16 = jnp.bfloat16
F8 = jnp.float8_e4m3fn

POOL_WINDOWS = (2, 4, 8, 16)
POOL_WIDTH = 256
POOL_GROUP_DIM = POOL_WIDTH // len(POOL_WINDOWS)
POOL_HALO = max(POOL_WINDOWS) // 2
DA_HEADS = 4
DA_HEAD_DIM = 64
DA_V_DIM = 2 * DA_HEAD_DIM
DA_WIDTH = DA_HEADS * DA_V_DIM
CA_HEADS = 4
CA_HEAD_DIM = 64
CA_WIDTH = CA_HEADS * CA_HEAD_DIM
ROPE_THETA = 500000.0
ROPE_DIM = DA_HEAD_DIM // 4
ROPE_HALF = ROPE_DIM // 2
NORM_EPS = 1e-6
LOG2E = math.log2(math.e)

V7X_LANES = 128
V7X_SUBLANES = 8
V7X_VMEM_LIMIT_BYTES = 60000 * 1024

TOKEN_TILE = 1024
FFN_PARTS = 4
FUSED_TILE = 512
INPROJ_PARTS = 2
WIDE_TILE = 1024
ATTN_Q_TILE = 512
ATTN_K_TILE = 2048
ATTN_ROW_CHUNK = 1024
BF16_SUBLANE_TILE = 2 * V7X_SUBLANES
VT_ROWS = DA_V_DIM + BF16_SUBLANE_TILE
V7X_MXU_DEPTH = 256
QK_STACK = V7X_MXU_DEPTH


def _compiler_params(n_grid_axes):
    return pltpu.CompilerParams(
        dimension_semantics=("arbitrary",) * n_grid_axes,
        vmem_limit_bytes=V7X_VMEM_LIMIT_BYTES,
    )


def _resident(block_shape, index_map):
    return pl.BlockSpec(block_shape, index_map, pipeline_mode=pl.Buffered(1))


def _rms_norm(x, g):
    ms = jnp.mean(x * x, axis=-1, keepdims=True)
    return x * lax.rsqrt(ms + NORM_EPS) * g


def _dot(a, b):
    return jnp.dot(a, b, preferred_element_type=F32)


def _dot_t(a_t, b):
    return lax.dot_general(a_t, b, (((0,), (0,)), ((), ())), preferred_element_type=F32)


def _rope_table_kernel(pos_ref, cos_ref, sin_ref):
    pos = pos_ref[...].astype(F32)
    lane = lax.broadcasted_iota(jnp.int32, pos.shape, 1)
    in_comp = lane % DA_HEAD_DIM
    freq_idx = lane % ROPE_HALF
    inv_freq = (np.float32(ROPE_THETA) ** (-np.arange(0, ROPE_DIM, 2, dtype=np.float32) / np.float32(ROPE_DIM)))
    inv = jnp.full(pos.shape, float(inv_freq[0]), F32)
    for j in range(1, ROPE_HALF):
        inv = jnp.where(freq_idx == j, float(inv_freq[j]), inv)
    ang = pos * inv
    rotary = in_comp < ROPE_DIM
    cos_ref[...] = jnp.where(rotary, jnp.cos(ang), 1.0)
    sin_v = jnp.sin(ang)
    sin_ref[...] = jnp.where(rotary, jnp.where(in_comp < ROPE_HALF, -sin_v, sin_v), 0.0)


def _rope_tables(positions):
    n = positions.size
    pos = jnp.broadcast_to(positions.reshape(n, 1), (n, V7X_LANES))
    spec = pl.BlockSpec((TOKEN_TILE, V7X_LANES), lambda i: (i, 0))
    return pl.pallas_call(
        _rope_table_kernel,
        grid=(n // TOKEN_TILE,),
        in_specs=[spec],
        out_specs=[spec, spec],
        out_shape=[jax.ShapeDtypeStruct((n, V7X_LANES), F32)] * 2,
        compiler_params=_compiler_params(1),
        name="rope_tables",
    )(pos)


def _ffn_rows(x, g_pre_ref, w_up_ref, w_down_ref, g_post_ref, d_ff):
    h = _rms_norm(x, g_pre_ref[...]).astype(BF16)
    ab = _dot(h, w_up_ref[...])
    a = ab[:, :d_ff]
    b = ab[:, d_ff:]
    act = (a * jax.nn.sigmoid(a) * b).astype(BF16)
    y = _dot(act, w_down_ref[...])
    return x + 0.5 * _rms_norm(y, g_post_ref[...])


def _ffn_kernel(x_ref, g_pre_ref, w_up_ref, w_down_ref, g_post_ref, o_ref, *, d_ff):
    rows_per_part = x_ref.shape[0] // FFN_PARTS
    for part in range(FFN_PARTS):
        rows = slice(part * rows_per_part, (part + 1) * rows_per_part)
        o_ref[rows, :] = _ffn_rows(x_ref[rows, :], g_pre_ref, w_up_ref, w_down_ref, g_post_ref, d_ff)


def _ffn(x, g_pre, w_up, w_down, g_post, layer):
    n, d = x.shape
    d_ff = w_down.shape[1]
    row = pl.BlockSpec((TOKEN_TILE, d), lambda i: (i, 0))
    gain = _resident((None, 1, d), lambda i: (layer, 0, 0))
    return pl.pallas_call(
        functools.partial(_ffn_kernel, d_ff=d_ff),
        grid=(n // TOKEN_TILE,),
        in_specs=[
            row,
            gain,
            _resident((None, d, 2 * d_ff), lambda i: (layer, 0, 0)),
            _resident((None, d_ff, d), lambda i: (layer, 0, 0)),
            gain,
        ],
        out_specs=row,
        out_shape=jax.ShapeDtypeStruct((n, d), F32),
        compiler_params=_compiler_params(1),
        name="ffn",
    )(x, g_pre, w_up, w_down, g_post)


def _apply_rope(t, cos_t, sin_t):
    width = t.shape[-1]
    lane = lax.broadcasted_iota(jnp.int32, t.shape, 1) % DA_HEAD_DIM
    partner = jnp.where(lane < ROPE_HALF,
                        pltpu.roll(t, width - ROPE_HALF, 1),
                        pltpu.roll(t, ROPE_HALF, 1))
    reps = width // cos_t.shape[-1]
    cos_w = jnp.concatenate([cos_t] * reps, axis=-1)
    sin_w = jnp.concatenate([sin_t] * reps, axis=-1)
    return t * cos_w + partner * sin_w


def _split_fp8(x):
    fp8_max = float(jnp.finfo(F8).max)
    x = jnp.clip(x, -fp8_max, fp8_max)
    hi = x.astype(F8)
    return hi, x - hi.astype(F32)


def _ffn_inproj_kernel(x_ref, f_pre_ref, f_up_ref, f_down_ref, f_post_ref, g_ref, w_ref, cos_ref, sin_ref,
                       x_out_ref, u_ref, q8_ref, k8_ref, vt_ref, qca_ref, *, d_ff):
    rows_per_part = x_ref.shape[0] // INPROJ_PARTS
    for part in range(INPROJ_PARTS):
        rows = slice(part * rows_per_part, (part + 1) * rows_per_part)
        x = _ffn_rows(x_ref[rows, :], f_pre_ref, f_up_ref, f_down_ref, f_post_ref, d_ff)
        x_out_ref[rows, :] = x
        _inproj_rows(x, g_ref, w_ref, cos_ref, sin_ref, u_ref, q8_ref, k8_ref, vt_ref, qca_ref, rows)


def _inproj_rows(x, g_ref, w_ref, cos_ref, sin_ref, u_ref, q8_ref, k8_ref, vt_ref, qca_ref, rows):
    h = _rms_norm(x, g_ref[...]).astype(BF16)
    proj = _dot(h, w_ref[...])
    tm = proj.shape[0]
    o_q = POOL_WIDTH
    o_k = o_q + DA_WIDTH
    o_v = o_k + DA_WIDTH
    o_ca = o_v + DA_WIDTH
    cos_t = cos_ref[rows, :]
    sin_t = sin_ref[rows, :]
    u_ref[rows, :] = proj[:, :o_q]
    qk_scale = math.sqrt(DA_HEAD_DIM ** -0.5 * LOG2E)
    q_t = (_apply_rope(proj[:, o_q:o_k], cos_t, sin_t) * qk_scale).T
    k = _apply_rope(proj[:, o_k:o_v], cos_t, sin_t) * qk_scale

    first_half = lax.broadcasted_iota(jnp.int32, (tm, DA_V_DIM), 1) < DA_HEAD_DIM
    zero_rows = jnp.zeros((DA_HEAD_DIM, tm), F8)
    for hd in range(DA_HEADS):
        k_head = k[:, hd * DA_V_DIM:(hd + 1) * DA_V_DIM]
        k_swap = pltpu.roll(k_head, DA_HEAD_DIM, 1)
        for c in range(2):
            both = jnp.where(first_half, k_head, k_swap) if c == 0 else jnp.where(first_half, k_swap, k_head)
            hi, lo = _split_fp8(both)
            col = (2 * hd + c) * QK_STACK
            k8_ref[rows, col:col + DA_V_DIM] = hi
            k8_ref[rows, col + DA_V_DIM:col + QK_STACK] = jnp.where(first_half, lo, 0.0).astype(F8)
            row = hd * DA_V_DIM + c * DA_HEAD_DIM
            hi, lo = _split_fp8(q_t[row:row + DA_HEAD_DIM])
            base = c * QK_STACK
            q8_ref[hd, base:base + DA_HEAD_DIM, rows] = hi
            q8_ref[hd, base + DA_HEAD_DIM:base + 2 * DA_HEAD_DIM, rows] = lo.astype(F8)
            q8_ref[hd, base + 2 * DA_HEAD_DIM:base + 3 * DA_HEAD_DIM, rows] = hi
            q8_ref[hd, base + 3 * DA_HEAD_DIM:base + QK_STACK, rows] = zero_rows
    vt = proj[:, o_v:o_ca].T.astype(BF16)
    ones_row = (lax.broadcasted_iota(jnp.int32, (BF16_SUBLANE_TILE, vt.shape[1]), 0) == 0).astype(BF16)
    for hd in range(DA_HEADS):
        vt_ref[hd, :DA_V_DIM, rows] = vt[hd * DA_V_DIM:(hd + 1) * DA_V_DIM]
        vt_ref[hd, DA_V_DIM:, rows] = ones_row
    qca_ref[rows, :] = (proj[:, o_ca:] * (CA_HEAD_DIM ** -0.5 * LOG2E)).astype(BF16)


def _ffn_inproj(x, f_pre, f_up, f_down, f_post, g, w_in, cos_t, sin_t, layer, batch, seq):
    n, d = x.shape
    d_ff = f_down.shape[1]
    tm = min(FUSED_TILE, seq)
    tiles_per_seq = seq // tm
    row = lambda width: pl.BlockSpec((tm, width), lambda i: (i, 0))
    col = lambda rows: pl.BlockSpec((None, DA_HEADS, rows, tm),
                                    lambda i: (i // tiles_per_seq, 0, 0, i % tiles_per_seq))
    layer_block = lambda *shape: _resident((None,) + shape, lambda i: (layer,) + (0,) * len(shape))
    return pl.pallas_call(
        functools.partial(_ffn_inproj_kernel, d_ff=d_ff),
        grid=(n // tm,),
        in_specs=[
            row(d),
            layer_block(1, d),
            layer_block(d, 2 * d_ff),
            layer_block(d_ff, d),
            layer_block(1, d),
            layer_block(1, d),
            layer_block(d, w_in.shape[-1]),
            row(V7X_LANES),
            row(V7X_LANES),
        ],
        out_specs=[row(d), row(POOL_WIDTH), col(2 * QK_STACK), row(DA_HEADS * 2 * QK_STACK), col(VT_ROWS),
                   row(CA_WIDTH)],
        out_shape=[
            jax.ShapeDtypeStruct((n, d), F32),
            jax.ShapeDtypeStruct((n, POOL_WIDTH), F32),
            jax.ShapeDtypeStruct((batch, DA_HEADS, 2 * QK_STACK, seq), F8),
            jax.ShapeDtypeStruct((n, DA_HEADS * 2 * QK_STACK), F8),
            jax.ShapeDtypeStruct((batch, DA_HEADS, VT_ROWS, seq), BF16),
            jax.ShapeDtypeStruct((n, CA_WIDTH), BF16),
        ],
        compiler_params=_compiler_params(1),
        name="ffn_inproj",
    )(x, f_pre, f_up, f_down, f_post, g, w_in, cos_t, sin_t)


def _attn_kernel(q8_ref, k8_ref, vt_ref, lq1_ref, lk1_ref, lq2_ref, lk2_ref, g_ref, o_ref,
                 s_even, s_odd, acc_ref, *, lam_init, seq, tq, tk, chunk):
    n_blk = seq // tk
    n_steps = (seq // tq) * n_blk
    neg_inf_rows = jnp.full((V7X_SUBLANES, tq), -jnp.inf, F32)

    def score_chunk(q_start, k_start, r, s_buf, part_max):
        rows = pl.ds(pl.multiple_of(r, chunk), chunk)
        out = []
        for c in range(2):
            cols = slice(c * QK_STACK, (c + 1) * QK_STACK)
            s = _dot(k8_ref[pl.ds(pl.multiple_of(k_start + r, chunk), chunk), cols],
                     q8_ref[cols, pl.ds(q_start, tq)])
            s_buf[c, rows, :] = s
            s_max = jnp.max(s.reshape(-1, V7X_SUBLANES, tq), axis=0)
            out.append(jnp.maximum(part_max[c], s_max))
        return tuple(out)

    def softmax_chunk(k_start, r, s_buf, m_new, rescale):
        rows = pl.ds(pl.multiple_of(r, chunk), chunk)
        vt_rows = vt_ref[:, pl.ds(pl.multiple_of(k_start + r, chunk), chunk)]
        for c in range(2):
            p = jnp.exp2((s_buf[c, rows, :] - m_new[c]).astype(BF16))
            acc_ref[c] = acc_ref[c] * rescale[c] + _dot(vt_rows, p)

    def block_max(part_max):
        return tuple(jnp.max(pm, axis=0, keepdims=True) for pm in part_max)

    def step(t, t_next, s_cur, s_next, m, cur_max):
        k_blk = t % n_blk
        k_start = k_blk * tk
        next_q_start = pl.multiple_of((t_next // n_blk) * tq, tq)
        next_k_start = (t_next % n_blk) * tk
        m_new, alpha = [], []
        for c in range(2):
            m_prev = jnp.where(k_blk == 0, -jnp.inf, m[c])
            m_new.append(jnp.maximum(m_prev, cur_max[c]))
            alpha.append(jnp.exp2(m_prev - m_new[c]))
        m_new = tuple(m_new)

        def body(ci, part_max):
            r = ci * chunk
            part_max = score_chunk(next_q_start, next_k_start, r, s_next, part_max)
            softmax_chunk(k_start, r, s_cur, m_new, [jnp.where(ci == 0, a, 1.0) for a in alpha])
            return part_max

        part_max = lax.fori_loop(0, tk // chunk, body, (neg_inf_rows, neg_inf_rows))
        return m_new, block_max(part_max)

    lam = (jnp.exp(jnp.sum(lq1_ref[...] * lk1_ref[...], axis=-1, keepdims=True))
           - jnp.exp(jnp.sum(lq2_ref[...] * lk2_ref[...], axis=-1, keepdims=True))
           + lam_init)

    def finalize(q_tile):
        acc0 = acc_ref[0]
        acc1 = acc_ref[1]
        o_t = (acc0[:DA_V_DIM] * (1.0 / acc0[DA_V_DIM:DA_V_DIM + 1])
               - lam * (acc1[:DA_V_DIM] * (1.0 / acc1[DA_V_DIM:DA_V_DIM + 1])))
        ms = jnp.mean(o_t * o_t, axis=0, keepdims=True)
        o = o_t * lax.rsqrt(ms + NORM_EPS) * g_ref[...] * (1.0 - lam_init)
        o_ref[:, pl.ds(pl.multiple_of(q_tile * tq, tq), tq)] = o.astype(BF16)

    acc_ref[...] = jnp.zeros(acc_ref.shape, F32)
    first_max = lax.fori_loop(
        0, tk // chunk,
        lambda ci, pm: score_chunk(0, 0, ci * chunk, s_even, pm),
        (neg_inf_rows, neg_inf_rows))
    neg_inf = jnp.full((1, tq), -jnp.inf, F32)

    def pair(i, carry):
        m, max_even = carry
        t = 2 * i
        m, max_odd = step(t, t + 1, s_even, s_odd, m, max_even)
        m, max_even = step(t + 1, jnp.where(t + 2 == n_steps, 0, t + 2), s_odd, s_even, m, max_odd)

        @pl.when((t + 1) % n_blk == n_blk - 1)
        def _():
            finalize((t + 1) // n_blk)

        return m, max_even

    lax.fori_loop(0, n_steps // 2, pair, ((neg_inf, neg_inf), block_max(first_max)))


def _attention(q8, k8, vt, lq1, lk1, lq2, lk2, g, layer, lam_init):
    batch, _, _, seq = q8.shape
    k8 = k8.reshape(batch, seq, DA_HEADS * 2 * QK_STACK)
    tq = ATTN_Q_TILE
    tk = min(ATTN_K_TILE, seq // 2)
    chunk = min(ATTN_ROW_CHUNK, tk)
    assert seq % tq == 0 and seq % (2 * tk) == 0 and tk % chunk == 0
    small = lambda width: _resident((None, 1, width), lambda b, h: (layer, 0, 0))
    g = g.reshape(g.shape[0], DA_V_DIM, 1)
    return pl.pallas_call(
        functools.partial(_attn_kernel, lam_init=lam_init, seq=seq, tq=tq, tk=tk, chunk=chunk),
        grid=(batch, DA_HEADS),
        in_specs=[
            pl.BlockSpec((None, None, 2 * QK_STACK, seq), lambda b, h: (b, h, 0, 0)),
            pl.BlockSpec((None, seq, 2 * QK_STACK), lambda b, h: (b, 0, h)),
            pl.BlockSpec((None, None, VT_ROWS, seq), lambda b, h: (b, h, 0, 0)),
            small(DA_HEAD_DIM), small(DA_HEAD_DIM), small(DA_HEAD_DIM), small(DA_HEAD_DIM),
            _resident((None, DA_V_DIM, 1), lambda b, h: (layer, 0, 0)),
        ],
        out_specs=pl.BlockSpec((None, DA_V_DIM, seq), lambda b, h: (b, h, 0)),
        out_shape=jax.ShapeDtypeStruct((batch, DA_WIDTH, seq), BF16),
        scratch_shapes=[
            pltpu.VMEM((2, tk, tq), F32),
            pltpu.VMEM((2, tk, tq), F32),
            pltpu.VMEM((2, VT_ROWS, tq), F32),
        ],
        compiler_params=_compiler_params(2),
        name="diff_attention",
    )(q8, k8, vt, lq1, lk1, lq2, lk2, g)


def _mem_kv_kernel(mem_ref, g_ref, w_ref, kt_ref, v_ref):
    hm = _rms_norm(mem_ref[...], g_ref[...]).astype(BF16)
    kv = _dot(hm, w_ref[...])
    k_t = kv[:, :CA_WIDTH].T
    v = kv[:, CA_WIDTH:]
    k_head = lax.broadcasted_iota(jnp.int32, k_t.shape, 0) // CA_HEAD_DIM
    v_head = lax.broadcasted_iota(jnp.int32, v.shape, 1) // CA_HEAD_DIM
    for h in range(CA_HEADS):
        kt_ref[h] = jnp.where(k_head == h, k_t, 0.0).astype(BF16)
        v_ref[h] = jnp.where(v_head == h, v, 0.0).astype(BF16)


def _mem_kv(mem, g, w):
    batch, n_mem, d = mem.shape
    depth = w.shape[0]
    out_spec = lambda r, c: pl.BlockSpec((None, None, CA_HEADS, r, c), lambda l, b: (l, b, 0, 0, 0))
    return pl.pallas_call(
        _mem_kv_kernel,
        grid=(depth, batch),
        in_specs=[
            pl.BlockSpec((None, n_mem, d), lambda l, b: (b, 0, 0)),
            pl.BlockSpec((None, 1, d), lambda l, b: (l, 0, 0)),
            pl.BlockSpec((None, d, 2 * CA_WIDTH), lambda l, b: (l, 0, 0)),
        ],
        out_specs=[out_spec(CA_WIDTH, n_mem), out_spec(n_mem, CA_WIDTH)],
        out_shape=[
            jax.ShapeDtypeStruct((depth, batch, CA_HEADS, CA_WIDTH, n_mem), BF16),
            jax.ShapeDtypeStruct((depth, batch, CA_HEADS, n_mem, CA_WIDTH), BF16),
        ],
        compiler_params=_compiler_params(2),
        name="mem_kv",
    )(mem, g, w)


def _pooled(u_prev_ref, u_ref, u_next_ref, ext_ref, seq):
    tm = u_ref.shape[0]
    tiles_per_seq = seq // tm
    t = pl.program_id(0) % tiles_per_seq
    u = u_ref[...]
    ext_ref[pl.ds(0, POOL_HALO), :] = jnp.where(t > 0, u_prev_ref[...], 0.0)
    ext_ref[pl.ds(POOL_HALO, tm), :] = u
    ext_ref[pl.ds(POOL_HALO + tm, POOL_HALO), :] = jnp.where(t < tiles_per_seq - 1, u_next_ref[...], 0.0)

    def window(lo, hi, lanes):
        acc = ext_ref[pl.ds(POOL_HALO + lo, tm), lanes]
        for d in range(lo + 1, hi):
            acc = acc + ext_ref[pl.ds(POOL_HALO + d, tm), lanes]
        return acc

    groups_per_tile = V7X_LANES // POOL_GROUP_DIM
    shape = (tm, V7X_LANES)
    pos = t * tm + lax.broadcasted_iota(jnp.int32, shape, 0)
    group_in_tile = lax.broadcasted_iota(jnp.int32, shape, 1) // POOL_GROUP_DIM
    tiles = []
    for tile in range(POOL_WIDTH // V7X_LANES):
        lanes = slice(tile * V7X_LANES, (tile + 1) * V7X_LANES)
        total = jnp.zeros(shape, F32)
        count = jnp.ones(shape, F32)
        for gi in range(groups_per_tile):
            half = POOL_WINDOWS[tile * groups_per_tile + gi] // 2
            cnt = (jnp.minimum(pos + half, seq) - jnp.maximum(pos - half, 0)).astype(F32)
            total = jnp.where(group_in_tile == gi, window(-half, half, lanes), total)
            count = jnp.where(group_in_tile == gi, cnt, count)
        tiles.append(total / count)
    return jnp.concatenate(tiles, axis=-1) - u


def _merge_kernel(x_ref, u_prev_ref, u_ref, u_next_ref, o_da_ref, qca_ref, kt_ref, v_ref,
                  g_pre_ref, w_gate_ref, b_gate_ref, pool_w_ref, pool_scale_ref,
                  w_pool_ref, w_da_ref, w_ca_ref, w_out_ref, g_post_ref,
                  out_ref, ext_ref, *, seq):
    x = x_ref[...]
    d = x.shape[-1]
    h = _rms_norm(x, g_pre_ref[...]).astype(BF16)
    gates = jax.nn.sigmoid(_dot(h, w_gate_ref[...]) + b_gate_ref[...])

    pooled = _pooled(u_prev_ref, u_ref, u_next_ref, ext_ref, seq).astype(BF16)
    y_pool = (_dot(pooled, pool_w_ref[...]) * pool_scale_ref[...]).astype(BF16)

    qca = qca_ref[...]
    y_ca = jnp.zeros(qca.shape, F32)
    for hd in range(CA_HEADS):
        s = _dot(qca, kt_ref[hd])
        p = jnp.exp2(s - jnp.max(s, axis=-1, keepdims=True))
        p = p * (1.0 / jnp.sum(p, axis=-1, keepdims=True))
        y_ca = y_ca + _dot(p.astype(BF16), v_ref[hd])

    merged = (gates[:, :d] * _dot(y_pool, w_pool_ref[...])
              + gates[:, d:2 * d] * _dot_t(o_da_ref[...], w_da_ref[...])
              + gates[:, 2 * d:] * _dot(y_ca.astype(BF16), w_ca_ref[...]))
    y = _dot(merged.astype(BF16), w_out_ref[...])
    out_ref[...] = x + _rms_norm(y, g_post_ref[...])


def _merge(x, u, o_da, qca, kt, v, g_pre, w_gate, b_gate, pool_w, pool_scale,
           w_pool, w_da, w_ca, w_out, g_post, layer, seq):
    n, d = x.shape
    n_mem = v.shape[-2]
    tm = min(WIDE_TILE, seq)
    tiles_per_seq = seq // tm
    halo_blocks = tm // POOL_HALO
    n_halo = n // POOL_HALO
    row = lambda width: pl.BlockSpec((tm, width), lambda i: (i, 0))
    layer_block = lambda *shape: _resident((None,) + shape, lambda i: (layer,) + (0,) * len(shape))
    mem_block = lambda r, c: pl.BlockSpec((None, None, CA_HEADS, r, c),
                                          lambda i: (layer, i // tiles_per_seq, 0, 0, 0))
    return pl.pallas_call(
        functools.partial(_merge_kernel, seq=seq),
        grid=(n // tm,),
        in_specs=[
            row(d),
            pl.BlockSpec((POOL_HALO, POOL_WIDTH), lambda i: (jnp.maximum(i * halo_blocks - 1, 0), 0)),
            row(POOL_WIDTH),
            pl.BlockSpec((POOL_HALO, POOL_WIDTH), lambda i: (jnp.minimum((i + 1) * halo_blocks, n_halo - 1), 0)),
            pl.BlockSpec((None, DA_WIDTH, tm), lambda i: (i // tiles_per_seq, 0, i % tiles_per_seq)),
            row(CA_WIDTH),
            mem_block(CA_WIDTH, n_mem),
            mem_block(n_mem, CA_WIDTH),
            layer_block(1, d),
            layer_block(d, 3 * d),
            layer_block(1, 3 * d),
            layer_block(POOL_WIDTH, POOL_WIDTH),
            layer_block(1, POOL_WIDTH),
            layer_block(POOL_WIDTH, d),
            layer_block(DA_WIDTH, d),
            layer_block(CA_WIDTH, d),
            layer_block(d, d),
            layer_block(1, d),
        ],
        out_specs=row(d),
        out_shape=jax.ShapeDtypeStruct((n, d), F32),
        scratch_shapes=[pltpu.VMEM((tm + 2 * POOL_HALO, POOL_WIDTH), F32)],
        compiler_params=_compiler_params(1),
        name="merge",
    )(x, u, u, u, o_da, qca, kt, v, g_pre, w_gate, b_gate, pool_w, pool_scale,
      w_pool, w_da, w_ca, w_out, g_post)


def _block_diag(w):
    depth, groups, c, _ = w.shape
    eye = jnp.eye(groups, dtype=w.dtype)
    return jnp.einsum("lgcd,gh->lgchd", w, eye).reshape(depth, groups * c, groups * c)


def kernel(x, mem, positions, ffn1_pre_g, ffn1_w_up, ffn1_w_down, ffn1_post_g, mix_pre_g, w_in, pool_w, pool_scale, da_lambda_q1, da_lambda_k1, da_lambda_q2, da_lambda_k2, da_subln_g, mem_norm_g, w_mem_kv, w_gate, b_gate, w_br_pool, w_br_da, w_br_ca, w_out, mix_post_g, ffn2_pre_g, ffn2_w_up, ffn2_w_down, ffn2_post_g):
    batch, seq, d = x.shape
    depth = w_in.shape[0]
    assert seq % TOKEN_TILE == 0 and seq % min(WIDE_TILE, seq) == 0 and seq % ATTN_Q_TILE == 0
    assert w_in.shape[-1] == POOL_WIDTH + 3 * DA_WIDTH + CA_WIDTH

    vec = lambda a: a.astype(F32).reshape(depth, 1, a.shape[-1])
    bf = lambda a: a.astype(BF16)

    cos_t, sin_t = _rope_tables(positions)
    kt_all, v_all = _mem_kv(mem, vec(mem_norm_g), bf(w_mem_kv))

    ffn1 = (vec(ffn1_pre_g), bf(ffn1_w_up), bf(ffn1_w_down), vec(ffn1_post_g))
    ffn2 = (vec(ffn2_pre_g), bf(ffn2_w_up), bf(ffn2_w_down), vec(ffn2_post_g))
    mix_g, w_in_b = vec(mix_pre_g), bf(w_in)
    lam_vecs = tuple(vec(a) for a in (da_lambda_q1, da_lambda_k1, da_lambda_q2, da_lambda_k2))
    merge_params = (mix_g, bf(w_gate), vec(b_gate), bf(_block_diag(pool_w)), vec(pool_scale),
                    bf(w_br_pool), bf(w_br_da), bf(w_br_ca), bf(w_out), vec(mix_post_g))
    subln_g = vec(da_subln_g)

    xt = x.reshape(batch * seq, d)
    for layer in range(depth):
        lam_init = 0.8 - 0.6 * math.exp(-0.3 * layer)
        xt, u, q8, k8, vt, qca = _ffn_inproj(xt, *ffn1, mix_g, w_in_b, cos_t, sin_t, layer, batch, seq)
        o_da = _attention(q8, k8, vt, *lam_vecs, subln_g, layer, lam_init)
        xt = _merge(xt, u, o_da, qca, kt_all, v_all, *merge_params, layer, seq)
        xt = _ffn(xt, *ffn2, layer)
    return xt.reshape(batch, seq, d)
```
